```python
import jax, jax.numpy as jnp
from jax import lax
import numpy as np

D_MODEL = 2048
BATCH = 4
SEQ = 4096
DEPTH = 2

N_SELF = DEPTH // 2
N_CROSS = DEPTH - N_SELF
CONV_WIDTH = 3
HEAD_DIM = 64
N_HEADS = D_MODEL // HEAD_DIM
N_KV_HEADS = max(1, N_HEADS // 8)
GROUP = N_HEADS // N_KV_HEADS
WINDOW = 128
BLOCK = 128
ROT_DIM = HEAD_DIM // 4
ROPE_THETA = 500000.0
D_FF = 4 * D_MODEL
EPS = 1e-6

kernel_name = "yoco_shortconv_swa_sink_hybrid"


def rms_norm(x, g):
    xf = x.astype(jnp.float32)
    y = xf * lax.rsqrt(jnp.mean(xf * xf, axis=-1, keepdims=True) + EPS)
    return (y * g.astype(jnp.float32)).astype(x.dtype)


def ada_mod(c_act, w, b, n):
    m = c_act @ w + b
    return jnp.split(m[:, None, :], n, axis=-1)


def rope_tables(seq):
    inv = ROPE_THETA ** (-jnp.arange(0, ROT_DIM, 2, dtype=jnp.float32) / ROT_DIM)
    ang = jnp.arange(seq, dtype=jnp.float32)[:, None] * inv[None, :]
    return jnp.cos(ang), jnp.sin(ang)


def partial_rope(t, cos, sin):
    half = ROT_DIM // 2
    cos = cos[None, :, None, :].astype(t.dtype)
    sin = sin[None, :, None, :].astype(t.dtype)
    t1 = t[..., :half]
    t2 = t[..., half:ROT_DIM]
    return jnp.concatenate([t1 * cos - t2 * sin, t2 * cos + t1 * sin, t[..., ROT_DIM:]], axis=-1)


def short_conv_mixer(h, w_in, w_conv, w_out):
    b_gate, c_gate, u = jnp.split(h @ w_in, 3, axis=-1)
    z = c_gate * u
    z = lax.conv_general_dilated(
        z, w_conv[:, None, :].astype(z.dtype),
        window_strides=(1,), padding=[(CONV_WIDTH - 1, 0)],
        dimension_numbers=('NWC', 'WIO', 'NWC'), feature_group_count=D_MODEL)
    return (b_gate * z) @ w_out


def squared_relu_mlp(h, w_up, w_down):
    a = jax.nn.relu(h @ w_up)
    return (a * a) @ w_down


def shared_kv(h_res, c_act, kv_ada_w, kv_ada_b, kv_norm, w_kv, b_kv, cos, sin):
    shift, scale = ada_mod(c_act, kv_ada_w, kv_ada_b, 2)
    h = rms_norm(h_res, kv_norm) * (1 + scale) + shift
    b, s, _ = h.shape
    kv = (h @ w_kv + b_kv).reshape(b, s, 2, N_KV_HEADS, HEAD_DIM)
    k = partial_rope(kv[:, :, 0], cos, sin)
    v = kv[:, :, 1]
    return k, v


def band_blocks(t):
    b, s, nh, d = t.shape
    nb = s // BLOCK
    prev = jnp.pad(t, ((0, 0), (BLOCK, 0), (0, 0), (0, 0)))[:, :s]
    return jnp.concatenate([prev.reshape(b, nb, BLOCK, nh, d),
                            t.reshape(b, nb, BLOCK, nh, d)], axis=2)


def sliding_window_sink_attention(h, k, v, w_q, b_q, sinks, w_o, b_o, cos, sin):
    b, s, _ = h.shape
    nb = s // BLOCK
    q = partial_rope((h @ w_q + b_q).reshape(b, s, N_HEADS, HEAD_DIM), cos, sin)
    qb = q.reshape(b, nb, BLOCK, N_KV_HEADS, GROUP, HEAD_DIM)
    kb = band_blocks(k)
    vb = band_blocks(v)
    scores = jnp.einsum('bnqkgd,bnskd->bnkgqs', qb, kb).astype(jnp.float32) * (HEAD_DIM ** -0.5)
    blk = jnp.arange(nb)[:, None, None]
    q_pos = blk * BLOCK + jnp.arange(BLOCK)[None, :, None]
    k_pos = (blk - 1) * BLOCK + jnp.arange(2 * BLOCK)[None, None, :]
    valid = (k_pos <= q_pos) & (k_pos > q_pos - WINDOW) & (k_pos >= 0)
    scores = jnp.where(valid[None, :, None, None], scores, -jnp.inf)
    sink = jnp.broadcast_to(
        sinks.astype(jnp.float32).reshape(N_KV_HEADS, GROUP)[None, None, :, :, None, None],
        scores.shape[:-1] + (1,))
    probs = jax.nn.softmax(jnp.concatenate([scores, sink], axis=-1), axis=-1)[..., :-1]
    o = jnp.einsum('bnkgqs,bnskd->bnqkgd', probs.astype(vb.dtype), vb)
    return o.reshape(b, s, N_HEADS * HEAD_DIM) @ w_o + b_o


def setup_inputs(seed: int = 0) -> dict:
    key = jax.random.key(seed)
    ks = jax.random.split(key, 21)
    D = D_MODEL
    nrm = jax.random.normal
    kv_dim = 2 * N_KV_HEADS * HEAD_DIM
    q_dim = N_HEADS * HEAD_DIM
    return {
        "x": nrm(ks[0], (BATCH, SEQ, D), jnp.float32),
        "c": nrm(ks[1], (BATCH, D), jnp.float32),
        "ada_w": nrm(ks[2], (DEPTH, 2, D, 3 * D), jnp.float32) * (0.5 * D ** -0.5),
        "ada_b": nrm(ks[3], (DEPTH, 2, 3 * D), jnp.float32) * 0.02,
        "norm_pre": 1.0 + 0.1 * nrm(ks[4], (DEPTH, 2, D), jnp.float32),
        "norm_post": 1.0 + 0.1 * nrm(ks[5], (DEPTH, 2, D), jnp.float32),
        "conv_w_in": nrm(ks[6], (N_SELF, D, 3 * D), jnp.float32) * D ** -0.5,
        "conv_w": nrm(ks[7], (N_SELF, CONV_WIDTH, D), jnp.float32) * CONV_WIDTH ** -0.5,
        "conv_w_out": nrm(ks[8], (N_SELF, D, D), jnp.float32) * D ** -0.5,
        "kv_ada_w": nrm(ks[9], (D, 2 * D), jnp.float32) * (0.5 * D ** -0.5),
        "kv_ada_b": nrm(ks[10], (2 * D,), jnp.float32) * 0.02,
        "kv_norm": 1.0 + 0.1 * nrm(ks[11], (D,), jnp.float32),
        "w_kv": nrm(ks[12], (D, kv_dim), jnp.float32) * D ** -0.5,
        "b_kv": nrm(ks[13], (kv_dim,), jnp.float32) * 0.02,
        "w_q": nrm(ks[14], (N_CROSS, D, q_dim), jnp.float32) * D ** -0.5,
        "b_q": nrm(ks[15], (N_CROSS, q_dim), jnp.float32) * 0.02,
        "sinks": nrm(ks[16], (N_CROSS, N_HEADS), jnp.float32),
        "w_o": nrm(ks[17], (N_CROSS, q_dim, D), jnp.float32) * q_dim ** -0.5,
        "b_o": nrm(ks[18], (N_CROSS, D), jnp.float32) * 0.02,
        "mlp_up": nrm(ks[19], (DEPTH, D, D_FF), jnp.float32) * D ** -0.5,
        "mlp_down": nrm(ks[20], (DEPTH, D_FF, D), jnp.float32) * D_FF ** -0.5,
    }


def reference(x, c, ada_w, ada_b, norm_pre, norm_post, conv_w_in, conv_w, conv_w_out,
              kv_ada_w, kv_ada_b, kv_norm, w_kv, b_kv, w_q, b_q, sinks, w_o, b_o,
              mlp_up, mlp_down):
    cos, sin = rope_tables(x.shape[1])
    c_act = jax.nn.silu(c)
    k = v = None
    for l in range(DEPTH):
        shift, scale, gate = ada_mod(c_act, ada_w[l, 0], ada_b[l, 0], 3)
        h = rms_norm(x, norm_pre[l, 0]) * (1 + scale) + shift
        if l < N_SELF:
            y = short_conv_mixer(h, conv_w_in[l], conv_w[l], conv_w_out[l])
        else:
            if l == N_SELF:
                k, v = shared_kv(x, c_act, kv_ada_w, kv_ada_b, kv_norm, w_kv, b_kv, cos, sin)
            a = l - N_SELF
            y = sliding_window_sink_attention(h, k, v, w_q[a], b_q[a], sinks[a],
                                              w_o[a], b_o[a], cos, sin)
        x = x + gate * rms_norm(y, norm_post[l, 0])
        shift, scale, gate = ada_mod(c_act, ada_w[l, 1], ada_b[l, 1], 3)
        h = rms_norm(x, norm_pre[l, 1]) * (1 + scale) + shift
        x = x + gate * rms_norm(squared_relu_mlp(h, mlp_up[l], mlp_down[l]), norm_post[l, 1])
    return x
```

```python
import functools

import jax
import jax.numpy as jnp
from jax import lax
from jax.experimental import pallas as pl
from jax.experimental.pallas import tpu as pltpu

EPS = 1e-6
HEAD_DIM = 64
KV_GROUP = 8
ROT_DIM = HEAD_DIM // 4
ROPE_THETA = 500000.0
ATT_BLOCK = 128
LANES = 128
SUBLANES = 8
VMEM_LIMIT = 56 * 1024 * 1024

BF16 = jnp.bfloat16
F32 = jnp.float32


def _dot(a, b):
    return jnp.dot(a, b, preferred_element_type=F32)


def _rms_scale(x):
    return lax.rsqrt(jnp.mean(x * x, axis=-1, keepdims=True) + EPS)


def _params(sem):
    return pltpu.CompilerParams(dimension_semantics=sem, vmem_limit_bytes=VMEM_LIMIT)


def _ada_kernel(c_ref, w_ref, b_ref, o_ref):
    c = c_ref[...]
    c_act = c / (1.0 + jnp.exp(-c))
    o_ref[...] = _dot(c_act.astype(BF16), w_ref[...].astype(BF16)) + b_ref[...]


def _ada(c_pad, w, b, tn=512):
    g, d, n = w.shape
    return pl.pallas_call(
        _ada_kernel,
        grid=(g, n // tn),
        in_specs=[
            pl.BlockSpec((SUBLANES, d), lambda a, j: (0, 0)),
            pl.BlockSpec((None, d, tn), lambda a, j: (a, 0, j)),
            pl.BlockSpec((None, 1, tn), lambda a, j: (a, 0, j)),
        ],
        out_specs=pl.BlockSpec((None, SUBLANES, tn), lambda a, j: (a, 0, j)),
        out_shape=jax.ShapeDtypeStruct((g, SUBLANES, n), F32),
        compiler_params=_params(("parallel", "parallel")),
        name="ada_mod",
    )(c_pad, w, b)


def _mixer_kernel(x_ref, sh_ref, sc_ref, gt_ref, gpre_ref, gpost_ref, cw_ref,
                  wb_ref, wc_ref, wu_ref, wo_ref, out_ref,
                  h_ref, zbuf_ref, carry_ref, *, tiles_per_seq):
    i = pl.program_id(0)
    j = pl.program_id(1)
    nj = pl.num_programs(1)
    tm = x_ref.shape[0]

    @pl.when(j == 0)
    def _():
        x = x_ref[...]
        a = gpre_ref[...] * (1.0 + sc_ref[...])
        h_ref[...] = (x * _rms_scale(x) * a + sh_ref[...]).astype(BF16)

    h = h_ref[...]
    bg = _dot(h, wb_ref[...])
    z = _dot(h, wc_ref[...]) * _dot(h, wu_ref[...])

    seq_start = (i % tiles_per_seq) == 0
    zbuf_ref[0:SUBLANES, :] = jnp.where(seq_start, 0.0, carry_ref[j])
    zbuf_ref[SUBLANES:, :] = z
    carry_ref[j] = z[tm - SUBLANES:, :]
    cw = cw_ref[...]
    zc = (cw[2:3, :] * z
          + cw[1:2, :] * zbuf_ref[SUBLANES - 1:SUBLANES - 1 + tm, :]
          + cw[0:1, :] * zbuf_ref[SUBLANES - 2:SUBLANES - 2 + tm, :])
    contrib = _dot((bg * zc).astype(BF16), wo_ref[...])

    @pl.when(j == 0)
    def _():
        out_ref[...] = contrib

    @pl.when(j > 0)
    def _():
        out_ref[...] += contrib

    @pl.when(j == nj - 1)
    def _():
        y = out_ref[...]
        out_ref[...] = x_ref[...] + gt_ref[...] * (y * _rms_scale(y) * gpost_ref[...])


def _mixer(x, shift, scale, gate, g_pre, g_post, conv_w, w_in, w_out, seq, tm=512, tn=512):
    t, d = x.shape
    nj = d // tn
    tps = seq // tm
    mod_spec = pl.BlockSpec((None, 1, d), lambda i, j: (i // tps, 0, 0))
    vec_spec = pl.BlockSpec((1, d), lambda i, j: (0, 0))
    return pl.pallas_call(
        functools.partial(_mixer_kernel, tiles_per_seq=tps),
        grid=(t // tm, nj),
        in_specs=[
            pl.BlockSpec((tm, d), lambda i, j: (i, 0)),
            mod_spec, mod_spec, mod_spec, vec_spec, vec_spec,
            pl.BlockSpec((3, tn), lambda i, j: (0, j)),
            pl.BlockSpec((d, tn), lambda i, j: (0, j)),
            pl.BlockSpec((d, tn), lambda i, j: (0, nj + j)),
            pl.BlockSpec((d, tn), lambda i, j: (0, 2 * nj + j)),
            pl.BlockSpec((tn, d), lambda i, j: (j, 0)),
        ],
        out_specs=pl.BlockSpec((tm, d), lambda i, j: (i, 0)),
        out_shape=jax.ShapeDtypeStruct((t, d), F32),
        scratch_shapes=[
            pltpu.VMEM((tm, d), BF16),
            pltpu.VMEM((tm + SUBLANES, tn), F32),
            pltpu.VMEM((nj, SUBLANES, tn), F32),
        ],
        compiler_params=_params(("arbitrary", "arbitrary")),
        name="conv_mixer",
    )(x, shift, scale, gate, g_pre, g_post, conv_w, w_in, w_in, w_in, w_out)


def _mlp_kernel(x_ref, sh_ref, sc_ref, gt_ref, gpre_ref, gpost_ref,
                wup_ref, wdn_ref, out_ref, h_ref):
    j = pl.program_id(1)
    nj = pl.num_programs(1)

    @pl.when(j == 0)
    def _():
        x = x_ref[...]
        a = gpre_ref[...] * (1.0 + sc_ref[...])
        h_ref[...] = (x * _rms_scale(x) * a + sh_ref[...]).astype(BF16)

    a = jnp.maximum(_dot(h_ref[...], wup_ref[...]), 0.0)
    contrib = _dot((a * a).astype(BF16), wdn_ref[...])

    @pl.when(j == 0)
    def _():
        out_ref[...] = contrib

    @pl.when(j > 0)
    def _():
        out_ref[...] += contrib

    @pl.when(j == nj - 1)
    def _():
        y = out_ref[...]
        out_ref[...] = x_ref[...] + gt_ref[...] * (y * _rms_scale(y) * gpost_ref[...])


def _mlp(x, shift, scale, gate, g_pre, g_post, w_up, w_down, seq, tm=512, tf=1024):
    t, d = x.shape
    dff = w_up.shape[1]
    tps = seq // tm
    mod_spec = pl.BlockSpec((None, 1, d), lambda i, j: (i // tps, 0, 0))
    vec_spec = pl.BlockSpec((1, d), lambda i, j: (0, 0))
    return pl.pallas_call(
        _mlp_kernel,
        grid=(t // tm, dff // tf),
        in_specs=[
            pl.BlockSpec((tm, d), lambda i, j: (i, 0)),
            mod_spec, mod_spec, mod_spec, vec_spec, vec_spec,
            pl.BlockSpec((d, tf), lambda i, j: (0, j)),
            pl.BlockSpec((tf, d), lambda i, j: (j, 0)),
        ],
        out_specs=pl.BlockSpec((tm, d), lambda i, j: (i, 0)),
        out_shape=jax.ShapeDtypeStruct((t, d), F32),
        scratch_shapes=[pltpu.VMEM((tm, d), BF16)],
        compiler_params=_params(("parallel", "arbitrary")),
        name="relu2_mlp",
    )(x, shift, scale, gate, g_pre, g_post, w_up, w_down)


def _rope(t, cos_ref, s1_ref, s2_ref):
    n = t.shape[1]
    reps = n // LANES
    cos = jnp.tile(cos_ref[...], (1, reps))
    s1 = jnp.tile(s1_ref[...], (1, reps))
    s2 = jnp.tile(s2_ref[...], (1, reps))
    half = ROT_DIM // 2
    return (t * cos + pltpu.roll(t, n - half, 1) * s1 + pltpu.roll(t, half, 1) * s2)


def _qkv_kernel(x_ref, shq_ref, scq_ref, gq_ref, shk_ref, sck_ref, gk_ref,
                wq_ref, bq_ref, wk_ref, bk_ref, wv_ref, bv_ref,
                cos_ref, s1_ref, s2_ref, q_ref, k_ref, v_ref):
    x = x_ref[...]
    xn = x * _rms_scale(x)
    hq = (xn * (gq_ref[...] * (1.0 + scq_ref[...])) + shq_ref[...]).astype(BF16)
    hk = (xn * (gk_ref[...] * (1.0 + sck_ref[...])) + shk_ref[...]).astype(BF16)
    q = _dot(hq, wq_ref[...]) + bq_ref[...]
    q_ref[...] = (_rope(q, cos_ref, s1_ref, s2_ref) * (HEAD_DIM ** -0.5)).astype(BF16)
    k = _dot(hk, wk_ref[...]) + bk_ref[...]
    k_ref[...] = _rope(k, cos_ref, s1_ref, s2_ref).astype(BF16)
    v_ref[...] = (_dot(hk, wv_ref[...]) + bv_ref[...]).astype(BF16)


def _qkv(x, shq, scq, gq, shk, sck, gk, wq, bq, wk, bk, wv, bv, cos, s1, s2, seq, tm=512):
    t, d = x.shape
    nq = wq.shape[1]
    nk = wk.shape[1]
    tps = seq // tm
    mod_spec = pl.BlockSpec((None, 1, d), lambda i: (i // tps, 0, 0))
    vec_spec = pl.BlockSpec((1, d), lambda i: (0, 0))
    tab_spec = pl.BlockSpec((tm, LANES), lambda i: (i % tps, 0))

    def full(a):
        return pl.BlockSpec(a.shape, lambda i: (0, 0))

    return pl.pallas_call(
        _qkv_kernel,
        grid=(t // tm,),
        in_specs=[
            pl.BlockSpec((tm, d), lambda i: (i, 0)),
            mod_spec, mod_spec, vec_spec, mod_spec, mod_spec, vec_spec,
            full(wq), full(bq), full(wk), full(bk), full(wv), full(bv),
            tab_spec, tab_spec, tab_spec,
        ],
        out_specs=[
            pl.BlockSpec((tm, nq), lambda i: (i, 0)),
            pl.BlockSpec((tm, nk), lambda i: (i, 0)),
            pl.BlockSpec((tm, nk), lambda i: (i, 0)),
        ],
        out_shape=[
            jax.ShapeDtypeStruct((t, nq), BF16),
            jax.ShapeDtypeStruct((t, nk), BF16),
            jax.ShapeDtypeStruct((t, nk), BF16),
        ],
        compiler_params=_params(("parallel",)),
        name="qkv_proj",
    )(x, shq, scq, gq, shk, sck, gk, wq, bq, wk, bk, wv, bv, cos, s1, s2)


def _attn_kernel(sinks_ref, q_ref, kp_ref, kc_ref, vp_ref, vc_ref, o_ref):
    n = pl.program_id(1)
    blk = q_ref.shape[0]
    n_kv = kp_ref.shape[1] // LANES
    pairs_per_kv = KV_GROUP // 2

    qi = lax.broadcasted_iota(jnp.int32, (2 * blk, 2 * blk), 0) & (blk - 1)
    kj = lax.broadcasted_iota(jnp.int32, (2 * blk, 2 * blk), 1)
    valid2 = ((kj > qi) & (kj < blk) & (n > 0)) | ((kj >= blk) & (kj - blk <= qi))
    lo = lax.broadcasted_iota(jnp.int32, (blk, LANES), 1) < HEAD_DIM
    top = lax.broadcasted_iota(jnp.int32, (2 * blk, 1), 0) < blk

    for g in range(n_kv):
        cols = slice(g * LANES, (g + 1) * LANES)
        kband = jnp.concatenate([kp_ref[:, cols], kc_ref[:, cols]], axis=0)
        vband = jnp.concatenate([vp_ref[:, cols], vc_ref[:, cols]], axis=0)
        for pp in range(pairs_per_kv):
            p = g * pairs_per_kv + pp
            pcols = slice(p * LANES, (p + 1) * LANES)
            qp = q_ref[:, pcols]
            zero = jnp.zeros_like(qp)
            q2 = jnp.concatenate([jnp.where(lo, qp, zero), jnp.where(lo, zero, qp)], axis=0)
            s = lax.dot_general(q2, kband, (((1,), (1,)), ((), ())),
                                preferred_element_type=F32)
            s = jnp.where(valid2, s, -jnp.inf)
            sink = jnp.where(top, sinks_ref[2 * p], sinks_ref[2 * p + 1])
            m = jnp.maximum(jnp.max(s, axis=-1, keepdims=True), sink)
            e = jnp.exp(s - m)
            denom = jnp.sum(e, axis=-1, keepdims=True) + jnp.exp(sink - m)
            pv = _dot(e.astype(BF16), vband) * (1.0 / denom)
            o_ref[:, pcols] = jnp.where(lo, pv[:blk], pv[blk:]).astype(BF16)


def _attention(sinks, q, kd, vd, seq):
    t, nq = q.shape
    nk = kd.shape[1]
    nb = seq // ATT_BLOCK
    batch = t // seq
    cur = lambda b, n: (b * nb + n, 0)
    prev = lambda b, n: (b * nb + jnp.maximum(n - 1, 0), 0)
    return pl.pallas_call(
        _attn_kernel,
        grid=(batch, nb),
        in_specs=[
            pl.BlockSpec(memory_space=pltpu.SMEM),
            pl.BlockSpec((ATT_BLOCK, nq), cur),
            pl.BlockSpec((ATT_BLOCK, nk), prev),
            pl.BlockSpec((ATT_BLOCK, nk), cur),
            pl.BlockSpec((ATT_BLOCK, nk), prev),
            pl.BlockSpec((ATT_BLOCK, nk), cur),
        ],
        out_specs=pl.BlockSpec((ATT_BLOCK, nq), cur),
        out_shape=jax.ShapeDtypeStruct((t, nq), BF16),
        compiler_params=_params(("parallel", "parallel")),
        name="swa_attention",
    )(sinks, q, kd, kd, vd, vd)


def _oproj_kernel(x_ref, o_ref, gt_ref, gpost_ref, wo_ref, bo_ref, out_ref):
    y = _dot(o_ref[...], wo_ref[...]) + bo_ref[...]
    out_ref[...] = x_ref[...] + gt_ref[...] * (y * _rms_scale(y) * gpost_ref[...])


def _oproj(x, o, gate, g_post, w_o, b_o, seq, tm=512):
    t, d = x.shape
    nq = o.shape[1]
    tps = seq // tm
    return pl.pallas_call(
        _oproj_kernel,
        grid=(t // tm,),
        in_specs=[
            pl.BlockSpec((tm, d), lambda i: (i, 0)),
            pl.BlockSpec((tm, nq), lambda i: (i, 0)),
            pl.BlockSpec((None, 1, d), lambda i: (i // tps, 0, 0)),
            pl.BlockSpec((1, d), lambda i: (0, 0)),
            pl.BlockSpec((nq, d), lambda i: (0, 0)),
            pl.BlockSpec((1, d), lambda i: (0, 0)),
        ],
        out_specs=pl.BlockSpec((tm, d), lambda i: (i, 0)),
        out_shape=jax.ShapeDtypeStruct((t, d), F32),
        compiler_params=_params(("parallel",)),
        name="attn_out_proj",
    )(x, o, gate, g_post, w_o, b_o)


def _rope_tables(seq):
    half = ROT_DIM // 2
    inv = ROPE_THETA ** (-jnp.arange(0, ROT_DIM, 2, dtype=F32) / ROT_DIM)
    ang = jnp.arange(seq, dtype=F32)[:, None] * inv[None, :]
    cos, sin = jnp.cos(ang), jnp.sin(ang)
    ones = jnp.ones((seq, HEAD_DIM - ROT_DIM), F32)
    zeros = jnp.zeros((seq, HEAD_DIM - ROT_DIM), F32)
    zh = jnp.zeros((seq, half), F32)
    c_tab = jnp.concatenate([cos, cos, ones], axis=1)
    s1_tab = jnp.concatenate([-sin, zh, zeros], axis=1)
    s2_tab = jnp.concatenate([zh, sin, zeros], axis=1)
    rep = LANES // HEAD_DIM
    return tuple(jnp.tile(a, (1, rep)) for a in (c_tab, s1_tab, s2_tab))


def _dup_heads(w, n_heads):
    lead = w.shape[:-1]
    w = w.reshape(lead + (n_heads, 1, HEAD_DIM))
    w = jnp.broadcast_to(w, lead + (n_heads, LANES // HEAD_DIM, HEAD_DIM))
    return w.reshape(lead + (n_heads * LANES,))


def kernel(x, c, ada_w, ada_b, norm_pre, norm_post, conv_w_in, conv_w, conv_w_out,
           kv_ada_w, kv_ada_b, kv_norm, w_kv, b_kv, w_q, b_q, sinks, w_o, b_o,
           mlp_up, mlp_down):
    batch, seq, d = x.shape
    depth = ada_w.shape[0]
    n_self = conv_w_in.shape[0]
    n_kv = w_kv.shape[1] // (2 * HEAD_DIM)
    t = batch * seq
    assert depth - n_self == 1, "exactly one shared-KV attention layer is supported"

    c_pad = jnp.pad(c, ((0, SUBLANES - batch), (0, 0)))
    mods = _ada(c_pad, ada_w.reshape(depth * 2, d, 3 * d), ada_b.reshape(depth * 2, 1, 3 * d))
    mods = mods[:, :batch].reshape(depth, 2, batch, 3, 1, d)
    kv_mods = _ada(c_pad, kv_ada_w[None], kv_ada_b[None, None])
    kv_mods = kv_mods[0, :batch].reshape(batch, 2, 1, d)

    def mod(l, sub, which):
        return mods[l, sub, :, which]

    xf = x.reshape(t, d)
    for l in range(depth):
        g_pre = norm_pre[l]
        g_post = norm_post[l]
        if l < n_self:
            xf = _mixer(xf, mod(l, 0, 0), mod(l, 0, 1), mod(l, 0, 2),
                        g_pre[0:1], g_post[0:1], conv_w[l],
                        conv_w_in[l].astype(BF16), conv_w_out[l].astype(BF16), seq)
        else:
            a = l - n_self
            cos, s1, s2 = _rope_tables(seq)
            w_k = _dup_heads(w_kv[:, :n_kv * HEAD_DIM], n_kv).astype(BF16)
            w_v = _dup_heads(w_kv[:, n_kv * HEAD_DIM:], n_kv).astype(BF16)
            b_k = _dup_heads(b_kv[None, :n_kv * HEAD_DIM], n_kv)
            b_v = _dup_heads(b_kv[None, n_kv * HEAD_DIM:], n_kv)
            q, kd, vd = _qkv(xf, mod(l, 0, 0), mod(l, 0, 1), g_pre[0:1],
                             kv_mods[:, 0], kv_mods[:, 1], kv_norm[None],
                             w_q[a].astype(BF16), b_q[a][None], w_k, b_k, w_v, b_v,
                             cos, s1, s2, seq)
            o = _attention(sinks[a], q, kd, vd, seq)
            xf = _oproj(xf, o, mod(l, 0, 2), g_post[0:1], w_o[a].astype(BF16), b_o[a][None], seq)
        xf = _mlp(xf, mod(l, 1, 0), mod(l, 1, 1), mod(l, 1, 2), g_pre[1:2], g_post[1:2],
                  mlp_up[l].astype(BF16), mlp_down[l].astype(BF16), seq)
    return xf.reshape(batch, seq, d)
```

```python
import functools

import jax
import jax.numpy as jnp
from jax import lax
from jax.experimental import pallas as pl
from jax.experimental.pallas import tpu as pltpu

EPS = 1e-6
HEAD_DIM = 64
KV_GROUP = 8
ROT_DIM = HEAD_DIM // 4
ROPE_THETA = 500000.0
ATT_BLOCK = 128
LANES = 128
SUBLANES = 8
VMEM_LIMIT = 56 * 1024 * 1024

BF16 = jnp.bfloat16
F32 = jnp.float32


def _dot(a, b):
    return jnp.dot(a, b, preferred_element_type=F32)


def _rms_scale(x):
    return lax.rsqrt(jnp.mean(x * x, axis=-1, keepdims=True) + EPS)


def _params(sem):
    return pltpu.CompilerParams(dimension_semantics=sem, vmem_limit_bytes=VMEM_LIMIT)


ROW_CHUNK = 64


def _modulated_norm(x_ref, h_ref, acc_ref, r_ref, gpre_ref, sc_ref, sh_ref):
    r_ref[...] = jnp.broadcast_to(_rms_scale(x_ref[...]), r_ref.shape)
    a = gpre_ref[...] * (1.0 + sc_ref[...])
    sh = sh_ref[...]
    reps = x_ref.shape[1] // LANES

    def body(c, carry):
        rows = pl.ds(pl.multiple_of(c * ROW_CHUNK, ROW_CHUNK), ROW_CHUNK)
        r = jnp.tile(r_ref[rows, :], (1, reps))
        h_ref[rows, :] = (x_ref[rows, :] * r * a + sh).astype(BF16)
        acc_ref[rows, :] = jnp.zeros((ROW_CHUNK, acc_ref.shape[1]), F32)
        return carry

    lax.fori_loop(0, x_ref.shape[0] // ROW_CHUNK, body, 0)


def _gated_residual(x_ref, acc_ref, r_ref, gt_ref, gpost_ref):
    y = acc_ref[...]
    r_ref[...] = jnp.broadcast_to(_rms_scale(y), r_ref.shape)
    gg = gt_ref[...] * gpost_ref[...]
    reps = acc_ref.shape[1] // LANES

    def body(c, carry):
        rows = pl.ds(pl.multiple_of(c * ROW_CHUNK, ROW_CHUNK), ROW_CHUNK)
        r = jnp.tile(r_ref[rows, :], (1, reps))
        acc_ref[rows, :] = x_ref[rows, :] + acc_ref[rows, :] * r * gg
        return carry

    lax.fori_loop(0, x_ref.shape[0] // ROW_CHUNK, body, 0)


def _ada_kernel(c_ref, w_ref, b_ref, o_ref):
    c = c_ref[...]
    c_act = c / (1.0 + jnp.exp(-c))
    o_ref[...] = _dot(c_act.astype(BF16), w_ref[...].astype(BF16)) + b_ref[...]


def _ada(c_pad, w, b, tn=512):
    g, d, n = w.shape
    return pl.pallas_call(
        _ada_kernel,
        grid=(g, n // tn),
        in_specs=[
            pl.BlockSpec((SUBLANES, d), lambda a, j: (0, 0)),
            pl.BlockSpec((None, d, tn), lambda a, j: (a, 0, j)),
            pl.BlockSpec((None, 1, tn), lambda a, j: (a, 0, j)),
        ],
        out_specs=pl.BlockSpec((None, SUBLANES, tn), lambda a, j: (a, 0, j)),
        out_shape=jax.ShapeDtypeStruct((g, SUBLANES, n), F32),
        compiler_params=_params(("parallel", "parallel")),
        name="ada_mod",
    )(c_pad, w, b)


def _mixer_kernel(x_ref, sh_ref, sc_ref, gt_ref, gpre_ref, gpost_ref, cw_ref,
                  wb_ref, wc_ref, wu_ref, wo_ref, out_ref,
                  h_ref, r_ref, zbuf_ref, carry_ref, *, tiles_per_seq):
    i = pl.program_id(0)
    j = pl.program_id(1)
    nj = pl.num_programs(1)
    tm = x_ref.shape[0]

    @pl.when(j == 0)
    def _():
        _modulated_norm(x_ref, h_ref, out_ref, r_ref, gpre_ref, sc_ref, sh_ref)

    h = h_ref[...]
    bg = _dot(h, wb_ref[...])
    z = _dot(h, wc_ref[...]) * _dot(h, wu_ref[...])

    seq_start = (i % tiles_per_seq) == 0
    zbuf_ref[0:SUBLANES, :] = jnp.where(seq_start, 0.0, carry_ref[j])
    zbuf_ref[SUBLANES:, :] = z
    carry_ref[j] = z[tm - SUBLANES:, :]
    cw = cw_ref[...]
    zc = (cw[2:3, :] * z
          + cw[1:2, :] * zbuf_ref[SUBLANES - 1:SUBLANES - 1 + tm, :]
          + cw[0:1, :] * zbuf_ref[SUBLANES - 2:SUBLANES - 2 + tm, :])
    out_ref[...] += _dot((bg * zc).astype(BF16), wo_ref[...])

    @pl.when(j == nj - 1)
    def _():
        _gated_residual(x_ref, out_ref, r_ref, gt_ref, gpost_ref)


def _mixer(x, shift, scale, gate, g_pre, g_post, conv_w, w_in, w_out, seq, tm=512, tn=512):
    t, d = x.shape
    nj = d // tn
    tps = seq // tm
    mod_spec = pl.BlockSpec((None, 1, d), lambda i, j: (i // tps, 0, 0))
    vec_spec = pl.BlockSpec((1, d), lambda i, j: (0, 0))
    return pl.pallas_call(
        functools.partial(_mixer_kernel, tiles_per_seq=tps),
        grid=(t // tm, nj),
        in_specs=[
            pl.BlockSpec((tm, d), lambda i, j: (i, 0)),
            mod_spec, mod_spec, mod_spec, vec_spec, vec_spec,
            pl.BlockSpec((3, tn), lambda i, j: (0, j)),
            pl.BlockSpec((d, tn), lambda i, j: (0, j)),
            pl.BlockSpec((d, tn), lambda i, j: (0, nj + j)),
            pl.BlockSpec((d, tn), lambda i, j: (0, 2 * nj + j)),
            pl.BlockSpec((tn, d), lambda i, j: (j, 0)),
        ],
        out_specs=pl.BlockSpec((tm, d), lambda i, j: (i, 0)),
        out_shape=jax.ShapeDtypeStruct((t, d), F32),
        scratch_shapes=[
            pltpu.VMEM((tm, d), BF16),
            pltpu.VMEM((tm, LANES), F32),
            pltpu.VMEM((tm + SUBLANES, tn), F32),
            pltpu.VMEM((nj, SUBLANES, tn), F32),
        ],
        compiler_params=_params(("arbitrary", "arbitrary")),
        name="conv_mixer",
    )(x, shift, scale, gate, g_pre, g_post, conv_w, w_in, w_in, w_in, w_out)


def _mlp_kernel(x_ref, sh_ref, sc_ref, gt_ref, gpre_ref, gpost_ref,
                wup_ref, wdn_ref, out_ref, h_ref, r_ref):
    j = pl.program_id(1)
    nj = pl.num_programs(1)

    @pl.when(j == 0)
    def _():
        _modulated_norm(x_ref, h_ref, out_ref, r_ref, gpre_ref, sc_ref, sh_ref)

    a = jnp.maximum(_dot(h_ref[...], wup_ref[...]), 0.0)
    out_ref[...] += _dot((a * a).astype(BF16), wdn_ref[...])

    @pl.when(j == nj - 1)
    def _():
        _gated_residual(x_ref, out_ref, r_ref, gt_ref, gpost_ref)


def _mlp(x, shift, scale, gate, g_pre, g_post, w_up, w_down, seq, tm=512, tf=1024):
    t, d = x.shape
    dff = w_up.shape[1]
    tps = seq // tm
    mod_spec = pl.BlockSpec((None, 1, d), lambda i, j: (i // tps, 0, 0))
    vec_spec = pl.BlockSpec((1, d), lambda i, j: (0, 0))
    return pl.pallas_call(
        _mlp_kernel,
        grid=(t // tm, dff // tf),
        in_specs=[
            pl.BlockSpec((tm, d), lambda i, j: (i, 0)),
            mod_spec, mod_spec, mod_spec, vec_spec, vec_spec,
            pl.BlockSpec((d, tf), lambda i, j: (0, j)),
            pl.BlockSpec((tf, d), lambda i, j: (j, 0)),
        ],
        out_specs=pl.BlockSpec((tm, d), lambda i, j: (i, 0)),
        out_shape=jax.ShapeDtypeStruct((t, d), F32),
        scratch_shapes=[pltpu.VMEM((tm, d), BF16), pltpu.VMEM((tm, LANES), F32)],
        compiler_params=_params(("parallel", "arbitrary")),
        name="relu2_mlp",
    )(x, shift, scale, gate, g_pre, g_post, w_up, w_down)


def _rope(t, cos_ref, s1_ref, s2_ref):
    n = t.shape[1]
    reps = n // LANES
    cos = jnp.tile(cos_ref[...], (1, reps))
    s1 = jnp.tile(s1_ref[...], (1, reps))
    s2 = jnp.tile(s2_ref[...], (1, reps))
    half = ROT_DIM // 2
    return (t * cos + pltpu.roll(t, n - half, 1) * s1 + pltpu.roll(t, half, 1) * s2)


def _qkv_kernel(x_ref, shq_ref, scq_ref, gq_ref, shk_ref, sck_ref, gk_ref,
                wq_ref, bq_ref, wk_ref, bk_ref, wv_ref, bv_ref,
                cos_ref, s1_ref, s2_ref, q_ref, k_ref, v_ref):
    x = x_ref[...]
    xn = x * _rms_scale(x)
    hq = (xn * (gq_ref[...] * (1.0 + scq_ref[...])) + shq_ref[...]).astype(BF16)
    hk = (xn * (gk_ref[...] * (1.0 + sck_ref[...])) + shk_ref[...]).astype(BF16)
    q = _dot(hq, wq_ref[...]) + bq_ref[...]
    q_ref[...] = (_rope(q, cos_ref, s1_ref, s2_ref) * (HEAD_DIM ** -0.5)).astype(BF16)
    k = _dot(hk, wk_ref[...]) + bk_ref[...]
    k_ref[...] = _rope(k, cos_ref, s1_ref, s2_ref).astype(BF16)
    v_ref[...] = (_dot(hk, wv_ref[...]) + bv_ref[...]).astype(BF16)


def _qkv(x, shq, scq, gq, shk, sck, gk, wq, bq, wk, bk, wv, bv, cos, s1, s2, seq, tm=512):
    t, d = x.shape
    nq = wq.shape[1]
    nk = wk.shape[1]
    tps = seq // tm
    mod_spec = pl.BlockSpec((None, 1, d), lambda i: (i // tps, 0, 0))
    vec_spec = pl.BlockSpec((1, d), lambda i: (0, 0))
    tab_spec = pl.BlockSpec((tm, LANES), lambda i: (i % tps, 0))

    def full(a):
        return pl.BlockSpec(a.shape, lambda i: (0, 0))

    return pl.pallas_call(
        _qkv_kernel,
        grid=(t // tm,),
        in_specs=[
            pl.BlockSpec((tm, d), lambda i: (i, 0)),
            mod_spec, mod_spec, vec_spec, mod_spec, mod_spec, vec_spec,
            full(wq), full(bq), full(wk), full(bk), full(wv), full(bv),
            tab_spec, tab_spec, tab_spec,
        ],
        out_specs=[
            pl.BlockSpec((tm, nq), lambda i: (i, 0)),
            pl.BlockSpec((tm, nk), lambda i: (i, 0)),
            pl.BlockSpec((tm, nk), lambda i: (i, 0)),
        ],
        out_shape=[
            jax.ShapeDtypeStruct((t, nq), BF16),
            jax.ShapeDtypeStruct((t, nk), BF16),
            jax.ShapeDtypeStruct((t, nk), BF16),
        ],
        compiler_params=_params(("parallel",)),
        name="qkv_proj",
    )(x, shq, scq, gq, shk, sck, gk, wq, bq, wk, bk, wv, bv, cos, s1, s2)


def _attn_kernel(sinks_ref, q_ref, kp_ref, kc_ref, vp_ref, vc_ref, o_ref):
    n = pl.program_id(1)
    blk = q_ref.shape[0]
    n_kv = kp_ref.shape[1] // LANES
    pairs_per_kv = KV_GROUP // 2

    qi = lax.broadcasted_iota(jnp.int32, (2 * blk, 2 * blk), 0) & (blk - 1)
    kj = lax.broadcasted_iota(jnp.int32, (2 * blk, 2 * blk), 1)
    valid2 = ((kj > qi) & (kj < blk) & (n > 0)) | ((kj >= blk) & (kj - blk <= qi))
    lo = lax.broadcasted_iota(jnp.int32, (blk, LANES), 1) < HEAD_DIM
    top = lax.broadcasted_iota(jnp.int32, (2 * blk, 1), 0) < blk

    for g in range(n_kv):
        cols = slice(g * LANES, (g + 1) * LANES)
        kband = jnp.concatenate([kp_ref[:, cols], kc_ref[:, cols]], axis=0)
        vband = jnp.concatenate([vp_ref[:, cols], vc_ref[:, cols]], axis=0)
        for pp in range(pairs_per_kv):
            p = g * pairs_per_kv + pp
            pcols = slice(p * LANES, (p + 1) * LANES)
            qp = q_ref[:, pcols]
            zero = jnp.zeros_like(qp)
            q2 = jnp.concatenate([jnp.where(lo, qp, zero), jnp.where(lo, zero, qp)], axis=0)
            s = lax.dot_general(q2, kband, (((1,), (1,)), ((), ())),
                                preferred_element_type=F32)
            s = jnp.where(valid2, s, -jnp.inf)
            sink = jnp.where(top, sinks_ref[2 * p], sinks_ref[2 * p + 1])
            m = jnp.maximum(jnp.max(s, axis=-1, keepdims=True), sink)
            e = jnp.exp(s - m)
            denom = jnp.sum(e, axis=-1, keepdims=True) + jnp.exp(sink - m)
            pv = _dot(e.astype(BF16), vband) * (1.0 / denom)
            o_ref[:, pcols] = jnp.where(lo, pv[:blk], pv[blk:]).astype(BF16)


def _attention(sinks, q, kd, vd, seq):
    t, nq = q.shape
    nk = kd.shape[1]
    nb = seq // ATT_BLOCK
    batch = t // seq
    cur = lambda b, n: (b * nb + n, 0)
    prev = lambda b, n: (b * nb + jnp.maximum(n - 1, 0), 0)
    return pl.pallas_call(
        _attn_kernel,
        grid=(batch, nb),
        in_specs=[
            pl.BlockSpec(memory_space=pltpu.SMEM),
            pl.BlockSpec((ATT_BLOCK, nq), cur),
            pl.BlockSpec((ATT_BLOCK, nk), prev),
            pl.BlockSpec((ATT_BLOCK, nk), cur),
            pl.BlockSpec((ATT_BLOCK, nk), prev),
            pl.BlockSpec((ATT_BLOCK, nk), cur),
        ],
        out_specs=pl.BlockSpec((ATT_BLOCK, nq), cur),
        out_shape=jax.ShapeDtypeStruct((t, nq), BF16),
        compiler_params=_params(("parallel", "parallel")),
        name="swa_attention",
    )(sinks, q, kd, kd, vd, vd)


def _oproj_kernel(x_ref, o_ref, gt_ref, gpost_ref, wo_ref, bo_ref, out_ref):
    y = _dot(o_ref[...], wo_ref[...]) + bo_ref[...]
    out_ref[...] = x_ref[...] + gt_ref[...] * (y * _rms_scale(y) * gpost_ref[...])


def _oproj(x, o, gate, g_post, w_o, b_o, seq, tm=512):
    t, d = x.shape
    nq = o.shape[1]
    tps = seq // tm
    return pl.pallas_call(
        _oproj_kernel,
        grid=(t // tm,),
        in_specs=[
            pl.BlockSpec((tm, d), lambda i: (i, 0)),
            pl.BlockSpec((tm, nq), lambda i: (i, 0)),
            pl.BlockSpec((None, 1, d), lambda i: (i // tps, 0, 0)),
            pl.BlockSpec((1, d), lambda i: (0, 0)),
            pl.BlockSpec((nq, d), lambda i: (0, 0)),
            pl.BlockSpec((1, d), lambda i: (0, 0)),
        ],
        out_specs=pl.BlockSpec((tm, d), lambda i: (i, 0)),
        out_shape=jax.ShapeDtypeStruct((t, d), F32),
        compiler_params=_params(("parallel",)),
        name="attn_out_proj",
    )(x, o, gate, g_post, w_o, b_o)


def _rope_tables(seq):
    half = ROT_DIM // 2
    inv = ROPE_THETA ** (-jnp.arange(0, ROT_DIM, 2, dtype=F32) / ROT_DIM)
    ang = jnp.arange(seq, dtype=F32)[:, None] * inv[None, :]
    cos, sin = jnp.cos(ang), jnp.sin(ang)
    ones = jnp.ones((seq, HEAD_DIM - ROT_DIM), F32)
    zeros = jnp.zeros((seq, HEAD_DIM - ROT_DIM), F32)
    zh = jnp.zeros((seq, half), F32)
    c_tab = jnp.concatenate([cos, cos, ones], axis=1)
    s1_tab = jnp.concatenate([-sin, zh, zeros], axis=1)
    s2_tab = jnp.concatenate([zh, sin, zeros], axis=1)
    rep = LANES // HEAD_DIM
    return tuple(jnp.tile(a, (1, rep)) for a in (c_tab, s1_tab, s2_tab))


def _dup_heads(w, n_heads):
    lead = w.shape[:-1]
    w = w.reshape(lead + (n_heads, 1, HEAD_DIM))
    w = jnp.broadcast_to(w, lead + (n_heads, LANES // HEAD_DIM, HEAD_DIM))
    return w.reshape(lead + (n_heads * LANES,))


def kernel(x, c, ada_w, ada_b, norm_pre, norm_post, conv_w_in, conv_w, conv_w_out,
           kv_ada_w, kv_ada_b, kv_norm, w_kv, b_kv, w_q, b_q, sinks, w_o, b_o,
           mlp_up, mlp_down):
    batch, seq, d = x.shape
    depth = ada_w.shape[0]
    n_self = conv_w_in.shape[0]
    n_kv = w_kv.shape[1] // (2 * HEAD_DIM)
    t = batch * seq
    assert depth - n_self == 1, "exactly one shared-KV attention layer is supported"

    c_pad = jnp.pad(c, ((0, SUBLANES - batch), (0, 0)))
    mods = _ada(c_pad, ada_w.reshape(depth * 2, d, 3 * d), ada_b.reshape(depth * 2, 1, 3 * d))
    mods = mods[:, :batch].reshape(depth, 2, batch, 3, 1, d)
    kv_mods = _ada(c_pad, kv_ada_w[None], kv_ada_b[None, None])
    kv_mods = kv_mods[0, :batch].reshape(batch, 2, 1, d)

    def mod(l, sub, which):
        return mods[l, sub, :, which]

    xf = x.reshape(t, d)
    for l in range(depth):
        g_pre = norm_pre[l]
        g_post = norm_post[l]
        if l < n_self:
            xf = _mixer(xf, mod(l, 0, 0), mod(l, 0, 1), mod(l, 0, 2),
                        g_pre[0:1], g_post[0:1], conv_w[l],
                        conv_w_in[l].astype(BF16), conv_w_out[l].astype(BF16), seq)
        else:
            a = l - n_self
            cos, s1, s2 = _rope_tables(seq)
            w_k = _dup_heads(w_kv[:, :n_kv * HEAD_DIM], n_kv).astype(BF16)
            w_v = _dup_heads(w_kv[:, n_kv * HEAD_DIM:], n_kv).astype(BF16)
            b_k = _dup_heads(b_kv[None, :n_kv * HEAD_DIM], n_kv)
            b_v = _dup_heads(b_kv[None, n_kv * HEAD_DIM:], n_kv)
            q, kd, vd = _qkv(xf, mod(l, 0, 0), mod(l, 0, 1), g_pre[0:1],
                             kv_mods[:, 0], kv_mods[:, 1], kv_norm[None],
                             w_q[a].astype(BF16), b_q[a][None], w_k, b_k, w_v, b_v,
                             cos, s1, s2, seq)
            o = _attention(sinks[a], q, kd, vd, seq)
            xf = _oproj(xf, o, mod(l, 0, 2), g_post[0:1], w_o[a].astype(BF16), b_o[a][None], seq)
        xf = _mlp(xf, mod(l, 1, 0), mod(l, 1, 1), mod(l, 1, 2), g_pre[1:2], g_post[1:2],
                  mlp_up[l].astype(BF16), mlp_down[l].astype(BF16), seq)
    return xf.reshape(batch, seq, d)
```

```python
import functools

import jax
import jax.numpy as jnp
from jax import lax
from jax.experimental import pallas as pl
from jax.experimental.pallas import tpu as pltpu

EPS = 1e-6
HEAD_DIM = 64
KV_GROUP = 8
ROT_DIM = HEAD_DIM // 4
ROPE_THETA = 500000.0
ATT_BLOCK = 128
LANES = 128
SUBLANES = 8
VMEM_LIMIT = 56 * 1024 * 1024
ROW_SUB = 32

BF16 = jnp.bfloat16
F32 = jnp.float32


def _dot(a, b):
    return jnp.dot(a, b, preferred_element_type=F32)


def _rms_scale(x):
    return lax.rsqrt(jnp.mean(x * x, axis=-1, keepdims=True) + EPS)


def _params(sem):
    return pltpu.CompilerParams(dimension_semantics=sem, vmem_limit_bytes=VMEM_LIMIT)


def _ada_kernel(c_ref, w_ref, b_ref, o_ref):
    c = c_ref[...]
    c_act = c / (1.0 + jnp.exp(-c))
    o_ref[...] = _dot(c_act.astype(BF16), w_ref[...].astype(BF16)) + b_ref[...]


def _ada(c_pad, w, b, tn=512):
    g, d, n = w.shape
    return pl.pallas_call(
        _ada_kernel,
        grid=(g, n // tn),
        in_specs=[
            pl.BlockSpec((SUBLANES, d), lambda a, j: (0, 0)),
            pl.BlockSpec((None, d, tn), lambda a, j: (a, 0, j)),
            pl.BlockSpec((None, 1, tn), lambda a, j: (a, 0, j)),
        ],
        out_specs=pl.BlockSpec((None, SUBLANES, tn), lambda a, j: (a, 0, j)),
        out_shape=jax.ShapeDtypeStruct((g, SUBLANES, n), F32),
        compiler_params=_params(("parallel", "parallel")),
        name="ada_mod",
    )(c_pad, w, b)


def _sublayer_kernel(*refs, n_tiles, n_w, n_cast, step_fn):
    xn_ref, xr_ref, sh_ref, sc_ref, gt_ref, gpre_ref, gpost_ref = refs[:7]
    k = 7
    w_refs = refs[k:k + n_w]
    k += n_w
    cast_in = refs[k:k + n_cast]
    k += n_cast
    out_ref = refs[k]
    cast_out = refs[k + 1:k + 1 + n_cast]
    k += 1 + n_cast
    h_ref, acc_ref = refs[k:k + 2]
    extra = refs[k + 2:]

    i = pl.program_id(0)
    j = pl.program_id(1)
    rc, d = xn_ref.shape
    tm = h_ref.shape[0] // 2
    row0 = j * rc

    def tile_base(t):
        if isinstance(t, int):
            return (t % 2) * tm
        return pl.multiple_of((t % 2) * tm, tm)

    def rows(base, s, size):
        return pl.ds(pl.multiple_of(base + row0 + s, size), size)

    def normalise(base):
        a = gpre_ref[...] * (1.0 + sc_ref[...])
        sh = sh_ref[...]
        for s in range(0, rc, ROW_SUB):
            x = xn_ref[s:s + ROW_SUB, :]
            h_ref[rows(base, s, ROW_SUB), :] = (x * _rms_scale(x) * a + sh).astype(BF16)

    def finish(base):
        gg = gt_ref[...] * gpost_ref[...]
        for s in range(0, rc, ROW_SUB):
            y = acc_ref[rows(base, s, ROW_SUB), :]
            out_ref[s:s + ROW_SUB, :] = xr_ref[s:s + ROW_SUB, :] + y * _rms_scale(y) * gg

    def clear(base):
        acc_ref[rows(base, 0, rc), :] = jnp.zeros((rc, d), F32)

    @pl.when(i == 0)
    def _():
        normalise(tile_base(0))
        clear(tile_base(0))
        clear(tile_base(1))

    @pl.when((i >= 1) & (i <= n_tiles))
    def _():
        def side_work():
            finish(tile_base(i))
            clear(tile_base(i))
            normalise(tile_base(i))
            for src, dst in zip(cast_in, cast_out):
                dst[...] = src[...].astype(BF16)

        step_fn(i - 1, j, w_refs, h_ref, acc_ref, pl.ds(tile_base(i - 1), tm), extra, side_work)

    @pl.when(i == n_tiles + 1)
    def _():
        finish(tile_base(n_tiles - 1))


def _cast_view(w, steps):
    for width in (2048, 1024, 512, 256, 128):
        if w.size % (width * steps * 2 * SUBLANES) == 0:
            return w.reshape(w.size // width, width)
    raise ValueError(f"cannot split {w.shape} into {steps} cast blocks")


def _sublayer(x, shift, scale, gate, g_pre, g_post, weights, weight_specs, step_fn, casts,
              *, seq, tm, nj, extra_scratch, name):
    t, d = x.shape
    n = t // tm
    rc = tm // nj
    tps = seq // tm
    steps = n * nj

    def norm_tile(i):
        return jnp.minimum(i, n - 1)

    def done_tile(i):
        return jnp.clip(i - 2, 0, n - 1)

    def done_map(i, j):
        return (done_tile(i) * nj + jnp.where(i < 2, 0, j), 0)

    def wj(i, j):
        return jnp.where(i == 0, 0, jnp.where(i == n + 1, nj - 1, j))

    def cast_map(i, j):
        return (jnp.clip((i - 1) * nj + j, 0, steps - 1), 0)

    pre_mod = pl.BlockSpec((None, 1, d), lambda i, j: (norm_tile(i) // tps, 0, 0))
    post_mod = pl.BlockSpec((None, 1, d), lambda i, j: (done_tile(i) // tps, 0, 0))
    vec = pl.BlockSpec((1, d), lambda i, j: (0, 0))
    views = [_cast_view(w, steps) for w in casts]
    cast_specs = [pl.BlockSpec((v.shape[0] // steps, v.shape[1]), cast_map) for v in views]

    outs = pl.pallas_call(
        functools.partial(_sublayer_kernel, n_tiles=n, n_w=len(weights), n_cast=len(casts),
                          step_fn=step_fn),
        grid=(n + 2, nj),
        in_specs=[
            pl.BlockSpec((rc, d), lambda i, j: (norm_tile(i) * nj + j, 0)),
            pl.BlockSpec((rc, d), done_map),
            pre_mod, pre_mod, post_mod, vec, vec,
            *weight_specs(wj),
            *cast_specs,
        ],
        out_specs=[pl.BlockSpec((rc, d), done_map), *cast_specs],
        out_shape=[jax.ShapeDtypeStruct((t, d), F32)]
        + [jax.ShapeDtypeStruct(v.shape, BF16) for v in views],
        scratch_shapes=[
            pltpu.VMEM((2 * tm, d), BF16),
            pltpu.VMEM((2 * tm, d), F32),
            *extra_scratch,
        ],
        compiler_params=_params(("arbitrary", "arbitrary")),
        name=name,
    )(x, x, shift, scale, gate, g_pre, g_post, *weights, *views)
    return outs[0], [o.reshape(w.shape) for o, w in zip(outs[1:], casts)]


def _mixer_step(tile, j, w_refs, h_ref, acc_ref, tile_rows, extra, side_work, *, tiles_per_seq):
    cw_ref, wb_ref, wc_ref, wu_ref, wo_ref = w_refs
    zbuf_ref, carry_ref = extra
    tm = tile_rows.size
    bg = _dot(h_ref[tile_rows, :], wb_ref[...])
    side_work()
    h = h_ref[tile_rows, :]
    z = _dot(h, wc_ref[...]) * _dot(h, wu_ref[...])

    seq_start = (tile % tiles_per_seq) == 0
    zbuf_ref[0:SUBLANES, :] = jnp.where(seq_start, 0.0, carry_ref[j])
    zbuf_ref[SUBLANES:, :] = z
    carry_ref[j] = z[tm - SUBLANES:, :]
    cw = cw_ref[...]
    zc = (cw[2:3, :] * z
          + cw[1:2, :] * zbuf_ref[SUBLANES - 1:SUBLANES - 1 + tm, :]
          + cw[0:1, :] * zbuf_ref[SUBLANES - 2:SUBLANES - 2 + tm, :])
    acc_ref[tile_rows, :] += _dot((bg * zc).astype(BF16), wo_ref[...])


def _mixer(x, shift, scale, gate, g_pre, g_post, conv_w, w_in, w_out, casts, seq,
           tm=512, tn=512):
    d = x.shape[1]
    nj = d // tn

    def weight_specs(wj):
        return [
            pl.BlockSpec((3, tn), lambda i, j: (0, wj(i, j))),
            pl.BlockSpec((d, tn), lambda i, j: (0, wj(i, j))),
            pl.BlockSpec((d, tn), lambda i, j: (0, nj + wj(i, j))),
            pl.BlockSpec((d, tn), lambda i, j: (0, 2 * nj + wj(i, j))),
            pl.BlockSpec((tn, d), lambda i, j: (wj(i, j), 0)),
        ]

    return _sublayer(
        x, shift, scale, gate, g_pre, g_post, [conv_w, w_in, w_in, w_in, w_out], weight_specs,
        functools.partial(_mixer_step, tiles_per_seq=seq // tm), casts,
        seq=seq, tm=tm, nj=nj,
        extra_scratch=[pltpu.VMEM((tm + SUBLANES, tn), F32),
                       pltpu.VMEM((nj, SUBLANES, tn), F32)],
        name="conv_mixer")


MLP_SUB = 512


def _mlp_step(tile, j, w_refs, h_ref, acc_ref, tile_rows, extra, side_work):
    wup_ref, wdn_ref = w_refs
    for c in range(0, wup_ref.shape[1], MLP_SUB):
        if c == MLP_SUB:
            side_work()
        a = jnp.maximum(_dot(h_ref[tile_rows, :], wup_ref[:, c:c + MLP_SUB]), 0.0)
        acc_ref[tile_rows, :] += _dot((a * a).astype(BF16), wdn_ref[c:c + MLP_SUB, :])


def _mlp(x, shift, scale, gate, g_pre, g_post, w_up, w_down, casts, seq, tm=1024, tf=1024):
    d = x.shape[1]
    dff = w_up.shape[1]

    def weight_specs(wj):
        return [
            pl.BlockSpec((d, tf), lambda i, j: (0, wj(i, j))),
            pl.BlockSpec((tf, d), lambda i, j: (wj(i, j), 0)),
        ]

    return _sublayer(
        x, shift, scale, gate, g_pre, g_post, [w_up, w_down], weight_specs, _mlp_step, casts,
        seq=seq, tm=tm, nj=dff // tf, extra_scratch=[], name="relu2_mlp")


def _rope(t, cos_ref, s1_ref, s2_ref):
    n = t.shape[1]
    reps = n // LANES
    cos = jnp.tile(cos_ref[...], (1, reps))
    s1 = jnp.tile(s1_ref[...], (1, reps))
    s2 = jnp.tile(s2_ref[...], (1, reps))
    half = ROT_DIM // 2
    return (t * cos + pltpu.roll(t, n - half, 1) * s1 + pltpu.roll(t, half, 1) * s2)


def _qkv_kernel(x_ref, shq_ref, scq_ref, gq_ref, shk_ref, sck_ref, gk_ref,
                wq_ref, bq_ref, wk_ref, bk_ref, wv_ref, bv_ref,
                cos_ref, s1_ref, s2_ref, q_ref, k_ref, v_ref):
    x = x_ref[...]
    xn = x * _rms_scale(x)
    hq = (xn * (gq_ref[...] * (1.0 + scq_ref[...])) + shq_ref[...]).astype(BF16)
    hk = (xn * (gk_ref[...] * (1.0 + sck_ref[...])) + shk_ref[...]).astype(BF16)
    q = _dot(hq, wq_ref[...]) + bq_ref[...]
    q_ref[...] = (_rope(q, cos_ref, s1_ref, s2_ref) * (HEAD_DIM ** -0.5)).astype(BF16)
    k = _dot(hk, wk_ref[...]) + bk_ref[...]
    k_ref[...] = _rope(k, cos_ref, s1_ref, s2_ref).astype(BF16)
    v_ref[...] = (_dot(hk, wv_ref[...]) + bv_ref[...]).astype(BF16)


def _qkv(x, shq, scq, gq, shk, sck, gk, wq, bq, wk, bk, wv, bv, cos, s1, s2, seq, tm=512):
    t, d = x.shape
    nq = wq.shape[1]
    nk = wk.shape[1]
    tps = seq // tm
    mod_spec = pl.BlockSpec((None, 1, d), lambda i: (i // tps, 0, 0))
    vec_spec = pl.BlockSpec((1, d), lambda i: (0, 0))
    tab_spec = pl.BlockSpec((tm, LANES), lambda i: (i % tps, 0))

    def full(a):
        return pl.BlockSpec(a.shape, lambda i: (0, 0))

    return pl.pallas_call(
        _qkv_kernel,
        grid=(t // tm,),
        in_specs=[
            pl.BlockSpec((tm, d), lambda i: (i, 0)),
            mod_spec, mod_spec, vec_spec, mod_spec, mod_spec, vec_spec,
            full(wq), full(bq), full(wk), full(bk), full(wv), full(bv),
            tab_spec, tab_spec, tab_spec,
        ],
        out_specs=[
            pl.BlockSpec((tm, nq), lambda i: (i, 0)),
            pl.BlockSpec((tm, nk), lambda i: (i, 0)),
            pl.BlockSpec((tm, nk), lambda i: (i, 0)),
        ],
        out_shape=[
            jax.ShapeDtypeStruct((t, nq), BF16),
            jax.ShapeDtypeStruct((t, nk), BF16),
            jax.ShapeDtypeStruct((t, nk), BF16),
        ],
        compiler_params=_params(("parallel",)),
        name="qkv_proj",
    )(x, shq, scq, gq, shk, sck, gk, wq, bq, wk, bk, wv, bv, cos, s1, s2)


def _attn_kernel(sinks_ref, q_ref, kp_ref, kc_ref, vp_ref, vc_ref, o_ref, *, q_blocks):
    n = pl.program_id(1)
    blk = ATT_BLOCK
    n_kv = kp_ref.shape[1] // LANES
    pairs_per_kv = KV_GROUP // 2

    qi = lax.broadcasted_iota(jnp.int32, (2 * blk, LANES), 0) & (blk - 1)
    kc = lax.broadcasted_iota(jnp.int32, (2 * blk, LANES), 1)
    from_prev = kc > qi
    lo = lax.broadcasted_iota(jnp.int32, (blk, LANES), 1) < HEAD_DIM
    top = lax.broadcasted_iota(jnp.int32, (2 * blk, 1), 0) < blk
    ones = jnp.ones((blk, LANES), BF16)
    no_prev_bias = jnp.where(n == 0, -jnp.inf, 0.0)

    for g in range(n_kv):
        cols = slice(g * LANES, (g + 1) * LANES)
        for b in range(q_blocks):
            cur = slice(b * blk, (b + 1) * blk)
            if b == 0:
                k_prev, v_prev = kp_ref[:, cols], vp_ref[:, cols]
            else:
                prv = slice((b - 1) * blk, b * blk)
                k_prev, v_prev = kc_ref[prv, cols], vc_ref[prv, cols]
            kband = jnp.concatenate([k_prev, kc_ref[cur, cols]], axis=0)
            vext = jnp.concatenate(
                [jnp.concatenate([v_prev, ones], axis=1),
                 jnp.concatenate([vc_ref[cur, cols], ones], axis=1)], axis=0)
            for pp in range(pairs_per_kv):
                p = g * pairs_per_kv + pp
                pcols = slice(p * LANES, (p + 1) * LANES)
                qp = q_ref[cur, pcols]
                zero = jnp.zeros_like(qp)
                q2 = jnp.concatenate([jnp.where(lo, qp, zero), jnp.where(lo, zero, qp)], axis=0)
                s = lax.dot_general(q2, kband, (((1,), (1,)), ((), ())),
                                    preferred_element_type=F32)
                s_prev = s[:, :LANES]
                if b == 0:
                    s_prev = s_prev + no_prev_bias
                sc = jnp.where(from_prev, s_prev, s[:, LANES:])
                sink = jnp.where(top, sinks_ref[2 * p], sinks_ref[2 * p + 1])
                m = jnp.maximum(jnp.max(sc, axis=-1, keepdims=True), sink)
                e = jnp.exp(sc - m).astype(BF16)
                ez = jnp.zeros_like(e)
                pmat = jnp.concatenate([jnp.where(from_prev, e, ez),
                                        jnp.where(from_prev, ez, e)], axis=1)
                pv = _dot(pmat, vext)
                den = pv[:, LANES:] + jnp.exp(sink - m)
                num = jnp.where(lo, pv[:blk, :LANES], pv[blk:, :LANES])
                o_ref[cur, pcols] = (num / jnp.where(lo, den[:blk], den[blk:])).astype(BF16)


def _attention(sinks, q, kd, vd, seq, q_blocks=2):
    t, nq = q.shape
    nk = kd.shape[1]
    tq = q_blocks * ATT_BLOCK
    steps = seq // tq
    batch = t // seq
    cur = lambda b, n: (b * steps + n, 0)
    prev = lambda b, n: ((b * steps + n) * q_blocks - jnp.where(n == 0, 0, 1), 0)
    return pl.pallas_call(
        functools.partial(_attn_kernel, q_blocks=q_blocks),
        grid=(batch, steps),
        in_specs=[
            pl.BlockSpec(memory_space=pltpu.SMEM),
            pl.BlockSpec((tq, nq), cur),
            pl.BlockSpec((ATT_BLOCK, nk), prev),
            pl.BlockSpec((tq, nk), cur),
            pl.BlockSpec((ATT_BLOCK, nk), prev),
            pl.BlockSpec((tq, nk), cur),
        ],
        out_specs=pl.BlockSpec((tq, nq), cur),
        out_shape=jax.ShapeDtypeStruct((t, nq), BF16),
        compiler_params=_params(("parallel", "parallel")),
        name="swa_attention",
    )(sinks, q, kd, kd, vd, vd)


def _oproj_kernel(x_ref, o_ref, gt_ref, gpost_ref, wo_ref, bo_ref, out_ref):
    y = _dot(o_ref[...], wo_ref[...]) + bo_ref[...]
    out_ref[...] = x_ref[...] + gt_ref[...] * (y * _rms_scale(y) * gpost_ref[...])


def _oproj(x, o, gate, g_post, w_o, b_o, seq, tm=512):
    t, d = x.shape
    nq = o.shape[1]
    tps = seq // tm
    return pl.pallas_call(
        _oproj_kernel,
        grid=(t // tm,),
        in_specs=[
            pl.BlockSpec((tm, d), lambda i: (i, 0)),
            pl.BlockSpec((tm, nq), lambda i: (i, 0)),
            pl.BlockSpec((None, 1, d), lambda i: (i // tps, 0, 0)),
            pl.BlockSpec((1, d), lambda i: (0, 0)),
            pl.BlockSpec((nq, d), lambda i: (0, 0)),
            pl.BlockSpec((1, d), lambda i: (0, 0)),
        ],
        out_specs=pl.BlockSpec((tm, d), lambda i: (i, 0)),
        out_shape=jax.ShapeDtypeStruct((t, d), F32),
        compiler_params=_params(("parallel",)),
        name="attn_out_proj",
    )(x, o, gate, g_post, w_o, b_o)


def _rope_tables(seq):
    half = ROT_DIM // 2
    inv = ROPE_THETA ** (-jnp.arange(0, ROT_DIM, 2, dtype=F32) / ROT_DIM)
    ang = jnp.arange(seq, dtype=F32)[:, None] * inv[None, :]
    cos, sin = jnp.cos(ang), jnp.sin(ang)
    ones = jnp.ones((seq, HEAD_DIM - ROT_DIM), F32)
    zeros = jnp.zeros((seq, HEAD_DIM - ROT_DIM), F32)
    zh = jnp.zeros((seq, half), F32)
    c_tab = jnp.concatenate([cos, cos, ones], axis=1)
    s1_tab = jnp.concatenate([-sin, zh, zeros], axis=1)
    s2_tab = jnp.concatenate([zh, sin, zeros], axis=1)
    rep = LANES // HEAD_DIM
    return tuple(jnp.tile(a, (1, rep)) for a in (c_tab, s1_tab, s2_tab))


def _dup_heads(w, n_heads):
    lead = w.shape[:-1]
    w = w.reshape(lead + (n_heads, 1, HEAD_DIM))
    w = jnp.broadcast_to(w, lead + (n_heads, LANES // HEAD_DIM, HEAD_DIM))
    return w.reshape(lead + (n_heads * LANES,))


def kernel(x, c, ada_w, ada_b, norm_pre, norm_post, conv_w_in, conv_w, conv_w_out,
           kv_ada_w, kv_ada_b, kv_norm, w_kv, b_kv, w_q, b_q, sinks, w_o, b_o,
           mlp_up, mlp_down):
    batch, seq, d = x.shape
    depth = ada_w.shape[0]
    n_self = conv_w_in.shape[0]
    n_kv = w_kv.shape[1] // (2 * HEAD_DIM)
    t = batch * seq
    assert n_self == 1 and depth == 2, "one conv-mixer layer then one shared-KV attention layer"

    c_pad = jnp.pad(c, ((0, SUBLANES - batch), (0, 0)))
    mods = _ada(c_pad, ada_w.reshape(depth * 2, d, 3 * d), ada_b.reshape(depth * 2, 1, 3 * d))
    mods = mods[:, :batch].reshape(depth, 2, batch, 3, 1, d)
    kv_mods = _ada(c_pad, kv_ada_w[None], kv_ada_b[None, None])
    kv_mods = kv_mods[0, :batch].reshape(batch, 2, 1, d)

    def mod(l, sub):
        return [mods[l, sub, :, which] for which in range(3)]

    xf = x.reshape(t, d)

    xf, (up0, down0) = _mixer(
        xf, *mod(0, 0), norm_pre[0, 0:1], norm_post[0, 0:1], conv_w[0],
        conv_w_in[0].astype(BF16), conv_w_out[0].astype(BF16),
        [mlp_up[0], mlp_down[0]], seq)
    xf, (wq_bf, wo_bf, up1, down1) = _mlp(
        xf, *mod(0, 1), norm_pre[0, 1:2], norm_post[0, 1:2], up0, down0,
        [w_q[0], w_o[0], mlp_up[1], mlp_down[1]], seq)

    cos, s1, s2 = _rope_tables(seq)
    w_k = _dup_heads(w_kv[:, :n_kv * HEAD_DIM], n_kv).astype(BF16)
    w_v = _dup_heads(w_kv[:, n_kv * HEAD_DIM:], n_kv).astype(BF16)
    b_k = _dup_heads(b_kv[None, :n_kv * HEAD_DIM], n_kv)
    b_v = _dup_heads(b_kv[None, n_kv * HEAD_DIM:], n_kv)
    shift, scale, gate = mod(1, 0)
    q, kd, vd = _qkv(xf, shift, scale, norm_pre[1, 0:1],
                     kv_mods[:, 0], kv_mods[:, 1], kv_norm[None],
                     wq_bf, b_q[0][None], w_k, b_k, w_v, b_v, cos, s1, s2, seq)
    o = _attention(sinks[0], q, kd, vd, seq)
    xf = _oproj(xf, o, gate, norm_post[1, 0:1], wo_bf, b_o[0][None], seq)
    xf, _ = _mlp(xf, *mod(1, 1), norm_pre[1, 1:2], norm_post[1, 1:2], up1, down1, [], seq)
    return xf.reshape(batch, seq, d)
```

```python
import functools

import jax
import jax.numpy as jnp
from jax import lax
from jax.experimental import pallas as pl
from jax.experimental.pallas import tpu as pltpu

EPS = 1e-6
HEAD_DIM = 64
KV_GROUP = 8
ROT_DIM = HEAD_DIM // 4
ROPE_THETA = 500000.0
ATT_BLOCK = 128
LANES = 128
SUBLANES = 8
VMEM_LIMIT = 56 * 1024 * 1024
ROW_SUB = 32

BF16 = jnp.bfloat16
F32 = jnp.float32


def _dot(a, b):
    return jnp.dot(a, b, preferred_element_type=F32)


def _rms_scale(x):
    return lax.rsqrt(jnp.mean(x * x, axis=-1, keepdims=True) + EPS)


def _params(sem):
    return pltpu.CompilerParams(dimension_semantics=sem, vmem_limit_bytes=VMEM_LIMIT)


def _ada_kernel(c_ref, w_ref, b_ref, o_ref):
    c = c_ref[...]
    c_act = c / (1.0 + jnp.exp(-c))
    o_ref[...] = _dot(c_act.astype(BF16), w_ref[...].astype(BF16)) + b_ref[...]


def _ada(c_pad, w, b, tn=512):
    g, d, n = w.shape
    return pl.pallas_call(
        _ada_kernel,
        grid=(g, n // tn),
        in_specs=[
            pl.BlockSpec((SUBLANES, d), lambda a, j: (0, 0)),
            pl.BlockSpec((None, d, tn), lambda a, j: (a, 0, j)),
            pl.BlockSpec((None, 1, tn), lambda a, j: (a, 0, j)),
        ],
        out_specs=pl.BlockSpec((None, SUBLANES, tn), lambda a, j: (a, 0, j)),
        out_shape=jax.ShapeDtypeStruct((g, SUBLANES, n), F32),
        compiler_params=_params(("parallel", "parallel")),
        name="ada_mod",
    )(c_pad, w, b)


def _sublayer_kernel(*refs, n_tiles, n_w, n_cast, step_fn):
    xn_ref, xr_ref, sh_ref, sc_ref, gt_ref, gpre_ref, gpost_ref = refs[:7]
    k = 7
    w_refs = refs[k:k + n_w]
    k += n_w
    cast_in = refs[k:k + n_cast]
    k += n_cast
    out_ref = refs[k]
    cast_out = refs[k + 1:k + 1 + n_cast]
    k += 1 + n_cast
    h_ref, acc_ref = refs[k:k + 2]
    extra = refs[k + 2:]

    i = pl.program_id(0)
    j = pl.program_id(1)
    rc, d = xn_ref.shape
    tm = h_ref.shape[0] // 2
    row0 = j * rc

    def tile_base(t):
        if isinstance(t, int):
            return (t % 2) * tm
        return pl.multiple_of((t % 2) * tm, tm)

    def rows(base, s, size):
        return pl.ds(pl.multiple_of(base + row0 + s, size), size)

    def normalise(base):
        a = gpre_ref[...] * (1.0 + sc_ref[...])
        sh = sh_ref[...]
        for s in range(0, rc, ROW_SUB):
            x = xn_ref[s:s + ROW_SUB, :]
            h_ref[rows(base, s, ROW_SUB), :] = (x * _rms_scale(x) * a + sh).astype(BF16)

    def finish(base):
        gg = gt_ref[...] * gpost_ref[...]
        for s in range(0, rc, ROW_SUB):
            y = acc_ref[rows(base, s, ROW_SUB), :]
            out_ref[s:s + ROW_SUB, :] = xr_ref[s:s + ROW_SUB, :] + y * _rms_scale(y) * gg

    def clear(base):
        acc_ref[rows(base, 0, rc), :] = jnp.zeros((rc, d), F32)

    @pl.when(i == 0)
    def _():
        normalise(tile_base(0))
        clear(tile_base(0))
        clear(tile_base(1))

    @pl.when((i >= 1) & (i <= n_tiles))
    def _():
        def side_work():
            finish(tile_base(i))
            clear(tile_base(i))
            normalise(tile_base(i))
            for src, dst in zip(cast_in, cast_out):
                dst[...] = src[...].astype(BF16)

        step_fn(i - 1, j, w_refs, h_ref, acc_ref, pl.ds(tile_base(i - 1), tm), extra, side_work)

    @pl.when(i == n_tiles + 1)
    def _():
        finish(tile_base(n_tiles - 1))


def _sublayer(x, shift, scale, gate, g_pre, g_post, weights, weight_specs, step_fn, casts,
              *, seq, tm, nj, extra_scratch, name):
    t, d = x.shape
    n = t // tm
    rc = tm // nj
    tps = seq // tm
    steps = n * nj

    def norm_tile(i):
        return jnp.minimum(i, n - 1)

    def done_tile(i):
        return jnp.clip(i - 2, 0, n - 1)

    def done_map(i, j):
        return (done_tile(i) * nj + jnp.where(i < 2, 0, j), 0)

    def wj(i, j):
        return jnp.where(i == 0, 0, jnp.where(i == n + 1, nj - 1, j))

    def cast_step(i, j):
        return jnp.clip((i - 1) * nj + j, 0, steps - 1)

    pre_mod = pl.BlockSpec((None, 1, d), lambda i, j: (norm_tile(i) // tps, 0, 0))
    post_mod = pl.BlockSpec((None, 1, d), lambda i, j: (done_tile(i) // tps, 0, 0))
    vec = pl.BlockSpec((1, d), lambda i, j: (0, 0))
    cast_in_specs, cast_out_specs, cast_shapes = [], [], []
    for w, layer in casts:
        _, r, c = w.shape
        assert r % (steps * 2 * SUBLANES) == 0, "cast row blocks must be whole bf16 tiles"
        cast_in_specs.append(pl.BlockSpec(
            (None, r // steps, c), lambda i, j, layer=layer: (layer, cast_step(i, j), 0)))
        cast_out_specs.append(pl.BlockSpec((r // steps, c), lambda i, j: (cast_step(i, j), 0)))
        cast_shapes.append(jax.ShapeDtypeStruct((r, c), BF16))

    outs = pl.pallas_call(
        functools.partial(_sublayer_kernel, n_tiles=n, n_w=len(weights), n_cast=len(casts),
                          step_fn=step_fn),
        grid=(n + 2, nj),
        in_specs=[
            pl.BlockSpec((rc, d), lambda i, j: (norm_tile(i) * nj + j, 0)),
            pl.BlockSpec((rc, d), done_map),
            pre_mod, pre_mod, post_mod, vec, vec,
            *weight_specs(wj),
            *cast_in_specs,
        ],
        out_specs=[pl.BlockSpec((rc, d), done_map), *cast_out_specs],
        out_shape=[jax.ShapeDtypeStruct((t, d), F32), *cast_shapes],
        scratch_shapes=[
            pltpu.VMEM((2 * tm, d), BF16),
            pltpu.VMEM((2 * tm, d), F32),
            *extra_scratch,
        ],
        compiler_params=_params(("arbitrary", "arbitrary")),
        name=name,
    )(x, x, shift, scale, gate, g_pre, g_post, *weights, *[w for w, _ in casts])
    return outs[0], outs[1:]


def _mixer_step(tile, j, w_refs, h_ref, acc_ref, tile_rows, extra, side_work, *, tiles_per_seq):
    cw_ref, wb_ref, wc_ref, wu_ref, wo_ref = w_refs
    zbuf_ref, carry_ref = extra
    tm = tile_rows.size
    bg = _dot(h_ref[tile_rows, :], wb_ref[...])
    side_work()
    h = h_ref[tile_rows, :]
    z = _dot(h, wc_ref[...]) * _dot(h, wu_ref[...])

    seq_start = (tile % tiles_per_seq) == 0
    zbuf_ref[0:SUBLANES, :] = jnp.where(seq_start, 0.0, carry_ref[j])
    zbuf_ref[SUBLANES:, :] = z
    carry_ref[j] = z[tm - SUBLANES:, :]
    cw = cw_ref[...]
    zc = (cw[2:3, :] * z
          + cw[1:2, :] * zbuf_ref[SUBLANES - 1:SUBLANES - 1 + tm, :]
          + cw[0:1, :] * zbuf_ref[SUBLANES - 2:SUBLANES - 2 + tm, :])
    acc_ref[tile_rows, :] += _dot((bg * zc).astype(BF16), wo_ref[...])


def _mixer(x, shift, scale, gate, g_pre, g_post, conv_w, w_in, w_out, casts, seq,
           tm=512, tn=512):
    d = x.shape[1]
    nj = d // tn

    def weight_specs(wj):
        return [
            pl.BlockSpec((3, tn), lambda i, j: (0, wj(i, j))),
            pl.BlockSpec((d, tn), lambda i, j: (0, wj(i, j))),
            pl.BlockSpec((d, tn), lambda i, j: (0, nj + wj(i, j))),
            pl.BlockSpec((d, tn), lambda i, j: (0, 2 * nj + wj(i, j))),
            pl.BlockSpec((tn, d), lambda i, j: (wj(i, j), 0)),
        ]

    return _sublayer(
        x, shift, scale, gate, g_pre, g_post, [conv_w, w_in, w_in, w_in, w_out], weight_specs,
        functools.partial(_mixer_step, tiles_per_seq=seq // tm), casts,
        seq=seq, tm=tm, nj=nj,
        extra_scratch=[pltpu.VMEM((tm + SUBLANES, tn), F32),
                       pltpu.VMEM((nj, SUBLANES, tn), F32)],
        name="conv_mixer")


MLP_SUB = 512


def _mlp_step(tile, j, w_refs, h_ref, acc_ref, tile_rows, extra, side_work):
    wup_ref, wdn_ref = w_refs
    for c in range(0, wup_ref.shape[1], MLP_SUB):
        if c == MLP_SUB:
            side_work()
        a = jnp.maximum(_dot(h_ref[tile_rows, :], wup_ref[:, c:c + MLP_SUB]), 0.0)
        acc_ref[tile_rows, :] += _dot((a * a).astype(BF16), wdn_ref[c:c + MLP_SUB, :])


def _mlp(x, shift, scale, gate, g_pre, g_post, w_up, w_down, casts, seq, tm=1024, tf=1024):
    d = x.shape[1]
    dff = w_up.shape[1]

    def weight_specs(wj):
        return [
            pl.BlockSpec((d, tf), lambda i, j: (0, wj(i, j))),
            pl.BlockSpec((tf, d), lambda i, j: (wj(i, j), 0)),
        ]

    return _sublayer(
        x, shift, scale, gate, g_pre, g_post, [w_up, w_down], weight_specs, _mlp_step, casts,
        seq=seq, tm=tm, nj=dff // tf, extra_scratch=[], name="relu2_mlp")


def _rope(t, cos_ref, s1_ref, s2_ref):
    n = t.shape[1]
    reps = n // LANES
    cos = jnp.tile(cos_ref[...], (1, reps))
    s1 = jnp.tile(s1_ref[...], (1, reps))
    s2 = jnp.tile(s2_ref[...], (1, reps))
    half = ROT_DIM // 2
    return (t * cos + pltpu.roll(t, n - half, 1) * s1 + pltpu.roll(t, half, 1) * s2)


def _qkv_kernel(x_ref, shq_ref, scq_ref, gq_ref, shk_ref, sck_ref, gk_ref,
                wq_ref, bq_ref, wk_ref, bk_ref, wv_ref, bv_ref,
                cos_ref, s1_ref, s2_ref, q_ref, k_ref, v_ref):
    x = x_ref[...]
    xn = x * _rms_scale(x)
    hq = (xn * (gq_ref[...] * (1.0 + scq_ref[...])) + shq_ref[...]).astype(BF16)
    hk = (xn * (gk_ref[...] * (1.0 + sck_ref[...])) + shk_ref[...]).astype(BF16)
    q = _dot(hq, wq_ref[...]) + bq_ref[...]
    q_ref[...] = (_rope(q, cos_ref, s1_ref, s2_ref) * (HEAD_DIM ** -0.5)).astype(BF16)
    k = _dot(hk, wk_ref[...]) + bk_ref[...]
    k_ref[...] = _rope(k, cos_ref, s1_ref, s2_ref).astype(BF16)
    v_ref[...] = (_dot(hk, wv_ref[...]) + bv_ref[...]).astype(BF16)


def _qkv(x, shq, scq, gq, shk, sck, gk, wq, bq, wk, bk, wv, bv, cos, s1, s2, seq, tm=512):
    t, d = x.shape
    nq = wq.shape[1]
    nk = wk.shape[1]
    tps = seq // tm
    mod_spec = pl.BlockSpec((None, 1, d), lambda i: (i // tps, 0, 0))
    vec_spec = pl.BlockSpec((1, d), lambda i: (0, 0))
    tab_spec = pl.BlockSpec((tm, LANES), lambda i: (i % tps, 0))

    def full(a):
        return pl.BlockSpec(a.shape, lambda i: (0, 0))

    return pl.pallas_call(
        _qkv_kernel,
        grid=(t // tm,),
        in_specs=[
            pl.BlockSpec((tm, d), lambda i: (i, 0)),
            mod_spec, mod_spec, vec_spec, mod_spec, mod_spec, vec_spec,
            full(wq), full(bq), full(wk), full(bk), full(wv), full(bv),
            tab_spec, tab_spec, tab_spec,
        ],
        out_specs=[
            pl.BlockSpec((tm, nq), lambda i: (i, 0)),
            pl.BlockSpec((tm, nk), lambda i: (i, 0)),
            pl.BlockSpec((tm, nk), lambda i: (i, 0)),
        ],
        out_shape=[
            jax.ShapeDtypeStruct((t, nq), BF16),
            jax.ShapeDtypeStruct((t, nk), BF16),
            jax.ShapeDtypeStruct((t, nk), BF16),
        ],
        compiler_params=_params(("parallel",)),
        name="qkv_proj",
    )(x, shq, scq, gq, shk, sck, gk, wq, bq, wk, bk, wv, bv, cos, s1, s2)


def _attn_kernel(sinks_ref, q_ref, kp_ref, kc_ref, vp_ref, vc_ref, o_ref, *, q_blocks):
    n = pl.program_id(1)
    blk = ATT_BLOCK
    n_kv = kp_ref.shape[1] // LANES
    pairs_per_kv = KV_GROUP // 2

    qi = lax.broadcasted_iota(jnp.int32, (2 * blk, LANES), 0) & (blk - 1)
    kc = lax.broadcasted_iota(jnp.int32, (2 * blk, LANES), 1)
    from_prev = kc > qi
    lo = lax.broadcasted_iota(jnp.int32, (blk, LANES), 1) < HEAD_DIM
    top = lax.broadcasted_iota(jnp.int32, (2 * blk, 1), 0) < blk
    ones = jnp.ones((blk, LANES), BF16)
    no_prev_bias = jnp.where(n == 0, -jnp.inf, 0.0)

    for g in range(n_kv):
        cols = slice(g * LANES, (g + 1) * LANES)
        for b in range(q_blocks):
            cur = slice(b * blk, (b + 1) * blk)
            if b == 0:
                k_prev, v_prev = kp_ref[:, cols], vp_ref[:, cols]
            else:
                prv = slice((b - 1) * blk, b * blk)
                k_prev, v_prev = kc_ref[prv, cols], vc_ref[prv, cols]
            kband = jnp.concatenate([k_prev, kc_ref[cur, cols]], axis=0)
            vext = jnp.concatenate(
                [jnp.concatenate([v_prev, ones], axis=1),
                 jnp.concatenate([vc_ref[cur, cols], ones], axis=1)], axis=0)
            for pp in range(pairs_per_kv):
                p = g * pairs_per_kv + pp
                pcols = slice(p * LANES, (p + 1) * LANES)
                qp = q_ref[cur, pcols]
                zero = jnp.zeros_like(qp)
                q2 = jnp.concatenate([jnp.where(lo, qp, zero), jnp.where(lo, zero, qp)], axis=0)
                s = lax.dot_general(q2, kband, (((1,), (1,)), ((), ())),
                                    preferred_element_type=F32)
                s_prev = s[:, :LANES]
                if b == 0:
                    s_prev = s_prev + no_prev_bias
                sc = jnp.where(from_prev, s_prev, s[:, LANES:])
                sink = jnp.where(top, sinks_ref[2 * p], sinks_ref[2 * p + 1])
                m = jnp.maximum(jnp.max(sc, axis=-1, keepdims=True), sink)
                e = jnp.exp(sc - m).astype(BF16)
                ez = jnp.zeros_like(e)
                pmat = jnp.concatenate([jnp.where(from_prev, e, ez),
                                        jnp.where(from_prev, ez, e)], axis=1)
                pv = _dot(pmat, vext)
                den = pv[:, LANES:] + jnp.exp(sink - m)
                num = jnp.where(lo, pv[:blk, :LANES], pv[blk:, :LANES])
                o_ref[cur, pcols] = (num / jnp.where(lo, den[:blk], den[blk:])).astype(BF16)


def _attention(sinks, q, kd, vd, seq, q_blocks=2):
    t, nq = q.shape
    nk = kd.shape[1]
    tq = q_blocks * ATT_BLOCK
    steps = seq // tq
    batch = t // seq
    cur = lambda b, n: (b * steps + n, 0)
    prev = lambda b, n: ((b * steps + n) * q_blocks - jnp.where(n == 0, 0, 1), 0)
    return pl.pallas_call(
        functools.partial(_attn_kernel, q_blocks=q_blocks),
        grid=(batch, steps),
        in_specs=[
            pl.BlockSpec(memory_space=pltpu.SMEM),
            pl.BlockSpec((tq, nq), cur),
            pl.BlockSpec((ATT_BLOCK, nk), prev),
            pl.BlockSpec((tq, nk), cur),
            pl.BlockSpec((ATT_BLOCK, nk), prev),
            pl.BlockSpec((tq, nk), cur),
        ],
        out_specs=pl.BlockSpec((tq, nq), cur),
        out_shape=jax.ShapeDtypeStruct((t, nq), BF16),
        compiler_params=_params(("parallel", "parallel")),
        name="swa_attention",
    )(sinks, q, kd, kd, vd, vd)


def _oproj_kernel(x_ref, o_ref, gt_ref, gpost_ref, wo_ref, bo_ref, out_ref):
    y = _dot(o_ref[...], wo_ref[...]) + bo_ref[...]
    out_ref[...] = x_ref[...] + gt_ref[...] * (y * _rms_scale(y) * gpost_ref[...])


def _oproj(x, o, gate, g_post, w_o, b_o, seq, tm=512):
    t, d = x.shape
    nq = o.shape[1]
    tps = seq // tm
    return pl.pallas_call(
        _oproj_kernel,
        grid=(t // tm,),
        in_specs=[
            pl.BlockSpec((tm, d), lambda i: (i, 0)),
            pl.BlockSpec((tm, nq), lambda i: (i, 0)),
            pl.BlockSpec((None, 1, d), lambda i: (i // tps, 0, 0)),
            pl.BlockSpec((1, d), lambda i: (0, 0)),
            pl.BlockSpec((nq, d), lambda i: (0, 0)),
            pl.BlockSpec((1, d), lambda i: (0, 0)),
        ],
        out_specs=pl.BlockSpec((tm, d), lambda i: (i, 0)),
        out_shape=jax.ShapeDtypeStruct((t, d), F32),
        compiler_params=_params(("parallel",)),
        name="attn_out_proj",
    )(x, o, gate, g_post, w_o, b_o)


def _rope_tables(seq):
    half = ROT_DIM // 2
    inv = ROPE_THETA ** (-jnp.arange(0, ROT_DIM, 2, dtype=F32) / ROT_DIM)
    ang = jnp.arange(seq, dtype=F32)[:, None] * inv[None, :]
    cos, sin = jnp.cos(ang), jnp.sin(ang)
    ones = jnp.ones((seq, HEAD_DIM - ROT_DIM), F32)
    zeros = jnp.zeros((seq, HEAD_DIM - ROT_DIM), F32)
    zh = jnp.zeros((seq, half), F32)
    c_tab = jnp.concatenate([cos, cos, ones], axis=1)
    s1_tab = jnp.concatenate([-sin, zh, zeros], axis=1)
    s2_tab = jnp.concatenate([zh, sin, zeros], axis=1)
    rep = LANES // HEAD_DIM
    return tuple(jnp.tile(a, (1, rep)) for a in (c_tab, s1_tab, s2_tab))


def _dup_heads(w, n_heads):
    lead = w.shape[:-1]
    w = w.reshape(lead + (n_heads, 1, HEAD_DIM))
    w = jnp.broadcast_to(w, lead + (n_heads, LANES // HEAD_DIM, HEAD_DIM))
    return w.reshape(lead + (n_heads * LANES,))


def kernel(x, c, ada_w, ada_b, norm_pre, norm_post, conv_w_in, conv_w, conv_w_out,
           kv_ada_w, kv_ada_b, kv_norm, w_kv, b_kv, w_q, b_q, sinks, w_o, b_o,
           mlp_up, mlp_down):
    batch, seq, d = x.shape
    depth = ada_w.shape[0]
    n_self = conv_w_in.shape[0]
    n_kv = w_kv.shape[1] // (2 * HEAD_DIM)
    t = batch * seq
    assert n_self == 1 and depth == 2, "one conv-mixer layer then one shared-KV attention layer"

    c_pad = jnp.pad(c, ((0, SUBLANES - batch), (0, 0)))
    mods = _ada(c_pad, ada_w.reshape(depth * 2, d, 3 * d), ada_b.reshape(depth * 2, 1, 3 * d))
    mods = mods[:, :batch].reshape(depth, 2, batch, 3, 1, d)
    kv_mods = _ada(c_pad, kv_ada_w[None], kv_ada_b[None, None])
    kv_mods = kv_mods[0, :batch].reshape(batch, 2, 1, d)

    def mod(l, sub):
        return [mods[l, sub, :, which] for which in range(3)]

    xf = x.reshape(t, d)

    xf, (up0, down0) = _mixer(
        xf, *mod(0, 0), norm_pre[0, 0:1], norm_post[0, 0:1], conv_w[0],
        conv_w_in[0].astype(BF16), conv_w_out[0].astype(BF16),
        [(mlp_up, 0), (mlp_down, 0)], seq)
    xf, (wq_bf, wo_bf, up1, down1) = _mlp(
        xf, *mod(0, 1), norm_pre[0, 1:2], norm_post[0, 1:2], up0, down0,
        [(w_q, 0), (w_o, 0), (mlp_up, 1), (mlp_down, 1)], seq)

    cos, s1, s2 = _rope_tables(seq)
    w_k = _dup_heads(w_kv[:, :n_kv * HEAD_DIM], n_kv).astype(BF16)
    w_v = _dup_heads(w_kv[:, n_kv * HEAD_DIM:], n_kv).astype(BF16)
    b_k = _dup_heads(b_kv[None, :n_kv * HEAD_DIM], n_kv)
    b_v = _dup_heads(b_kv[None, n_kv * HEAD_DIM:], n_kv)
    shift, scale, gate = mod(1, 0)
    q, kd, vd = _qkv(xf, shift, scale, norm_pre[1, 0:1],
                     kv_mods[:, 0], kv_mods[:, 1], kv_norm[None],
                     wq_bf, b_q[0][None], w_k, b_k, w_v, b_v, cos, s1, s2, seq)
    o = _attention(sinks[0], q, kd, vd, seq)
    xf = _oproj(xf, o, gate, norm_post[1, 0:1], wo_bf, b_o[0][None], seq)
    xf, _ = _mlp(xf, *mod(1, 1), norm_pre[1, 1:2], norm_post[1, 1:2], up1, down1, [], seq)
    return xf.reshape(batch, seq, d)
```

```python
import functools

import jax
import jax.numpy as jnp
from jax import lax
from jax.experimental import pallas as pl
from jax.experimental.pallas import tpu as pltpu

EPS = 1e-6
HEAD_DIM = 64
KV_GROUP = 8
ROT_DIM = HEAD_DIM // 4
ROPE_THETA = 500000.0
ATT_BLOCK = 128
LANES = 128
SUBLANES = 8
VMEM_LIMIT = 56 * 1024 * 1024
ROW_SUB = 32

BF16 = jnp.bfloat16
F32 = jnp.float32


def _dot(a, b):
    return jnp.dot(a, b, preferred_element_type=F32)


def _rms_scale(x):
    return lax.rsqrt(jnp.mean(x * x, axis=-1, keepdims=True) + EPS)


def _params(sem):
    return pltpu.CompilerParams(dimension_semantics=sem, vmem_limit_bytes=VMEM_LIMIT)


def _ada_kernel(c_ref, w_ref, b_ref, o_ref):
    c = c_ref[...]
    c_act = c / (1.0 + jnp.exp(-c))
    o_ref[...] = _dot(c_act.astype(BF16), w_ref[...].astype(BF16)) + b_ref[...]


def _ada(c_pad, w, b, tn=512):
    g, d, n = w.shape
    return pl.pallas_call(
        _ada_kernel,
        grid=(g, n // tn),
        in_specs=[
            pl.BlockSpec((SUBLANES, d), lambda a, j: (0, 0)),
            pl.BlockSpec((None, d, tn), lambda a, j: (a, 0, j)),
            pl.BlockSpec((None, 1, tn), lambda a, j: (a, 0, j)),
        ],
        out_specs=pl.BlockSpec((None, SUBLANES, tn), lambda a, j: (a, 0, j)),
        out_shape=jax.ShapeDtypeStruct((g, SUBLANES, n), F32),
        compiler_params=_params(("parallel", "parallel")),
        name="ada_mod",
    )(c_pad, w, b)


def _sublayer_kernel(*refs, n_tiles, n_w, n_cast, step_fn):
    xn_ref, xr_ref, sh_ref, sc_ref, gt_ref, gpre_ref, gpost_ref = refs[:7]
    k = 7
    w_refs = refs[k:k + n_w]
    k += n_w
    cast_in = refs[k:k + n_cast]
    k += n_cast
    out_ref = refs[k]
    cast_out = refs[k + 1:k + 1 + n_cast]
    k += 1 + n_cast
    h_ref, acc_ref = refs[k:k + 2]
    extra = refs[k + 2:]

    i = pl.program_id(0)
    j = pl.program_id(1)
    rc, d = xn_ref.shape
    tm = h_ref.shape[0] // 2
    row0 = j * rc

    def tile_base(t):
        if isinstance(t, int):
            return (t % 2) * tm
        return pl.multiple_of((t % 2) * tm, tm)

    def rows(base, s, size):
        return pl.ds(pl.multiple_of(base + row0 + s, size), size)

    def normalise(base):
        a = gpre_ref[...] * (1.0 + sc_ref[...])
        sh = sh_ref[...]
        for s in range(0, rc, ROW_SUB):
            x = xn_ref[s:s + ROW_SUB, :]
            h_ref[rows(base, s, ROW_SUB), :] = (x * _rms_scale(x) * a + sh).astype(BF16)

    def finish(base):
        gg = gt_ref[...] * gpost_ref[...]
        for s in range(0, rc, ROW_SUB):
            y = acc_ref[rows(base, s, ROW_SUB), :]
            out_ref[s:s + ROW_SUB, :] = xr_ref[s:s + ROW_SUB, :] + y * _rms_scale(y) * gg

    def clear(base):
        acc_ref[rows(base, 0, rc), :] = jnp.zeros((rc, d), F32)

    @pl.when(i == 0)
    def _():
        normalise(tile_base(0))
        clear(tile_base(0))
        clear(tile_base(1))

    @pl.when((i >= 1) & (i <= n_tiles))
    def _():
        def side_work():
            finish(tile_base(i))
            clear(tile_base(i))
            normalise(tile_base(i))
            for src, dst in zip(cast_in, cast_out):
                dst[...] = src[...].astype(BF16)

        step_fn(i - 1, j, w_refs, h_ref, acc_ref, pl.ds(tile_base(i - 1), tm), extra, side_work)

    @pl.when(i == n_tiles + 1)
    def _():
        finish(tile_base(n_tiles - 1))


def _sublayer(x, shift, scale, gate, g_pre, g_post, weights, weight_specs, step_fn, casts,
              *, seq, tm, nj, extra_scratch, name):
    t, d = x.shape
    n = t // tm
    rc = tm // nj
    tps = seq // tm
    steps = n * nj

    def norm_tile(i):
        return jnp.minimum(i, n - 1)

    def done_tile(i):
        return jnp.clip(i - 2, 0, n - 1)

    def done_map(i, j):
        return (done_tile(i) * nj + jnp.where(i < 2, 0, j), 0)

    def wj(i, j):
        return jnp.where(i == 0, 0, jnp.where(i == n + 1, nj - 1, j))

    def cast_step(i, j):
        return jnp.clip((i - 1) * nj + j, 0, steps - 1)

    pre_mod = pl.BlockSpec((None, 1, d), lambda i, j: (norm_tile(i) // tps, 0, 0))
    post_mod = pl.BlockSpec((None, 1, d), lambda i, j: (done_tile(i) // tps, 0, 0))
    vec = pl.BlockSpec((1, d), lambda i, j: (0, 0))
    cast_in_specs, cast_out_specs, cast_shapes = [], [], []
    for w, layer in casts:
        _, r, c = w.shape
        assert r % (steps * 2 * SUBLANES) == 0, "cast row blocks must be whole bf16 tiles"
        cast_in_specs.append(pl.BlockSpec(
            (None, r // steps, c), lambda i, j, layer=layer: (layer, cast_step(i, j), 0)))
        cast_out_specs.append(pl.BlockSpec((r // steps, c), lambda i, j: (cast_step(i, j), 0)))
        cast_shapes.append(jax.ShapeDtypeStruct((r, c), BF16))

    outs = pl.pallas_call(
        functools.partial(_sublayer_kernel, n_tiles=n, n_w=len(weights), n_cast=len(casts),
                          step_fn=step_fn),
        grid=(n + 2, nj),
        in_specs=[
            pl.BlockSpec((rc, d), lambda i, j: (norm_tile(i) * nj + j, 0)),
            pl.BlockSpec((rc, d), done_map),
            pre_mod, pre_mod, post_mod, vec, vec,
            *weight_specs(wj),
            *cast_in_specs,
        ],
        out_specs=[pl.BlockSpec((rc, d), done_map), *cast_out_specs],
        out_shape=[jax.ShapeDtypeStruct((t, d), F32), *cast_shapes],
        scratch_shapes=[
            pltpu.VMEM((2 * tm, d), BF16),
            pltpu.VMEM((2 * tm, d), F32),
            *extra_scratch,
        ],
        compiler_params=_params(("arbitrary", "arbitrary")),
        name=name,
    )(x, x, shift, scale, gate, g_pre, g_post, *weights, *[w for w, _ in casts])
    return outs[0], outs[1:]


def _mixer_step(tile, j, w_refs, h_ref, acc_ref, tile_rows, extra, side_work, *, tiles_per_seq):
    cw_ref, wb_ref, wc_ref, wu_ref, wo_ref = w_refs
    zbuf_ref, carry_ref = extra
    tm = tile_rows.size
    cg = _dot(h_ref[tile_rows, :], wc_ref[...])
    side_work()
    h = h_ref[tile_rows, :]
    z = cg * _dot(h, wu_ref[...])

    seq_start = (tile % tiles_per_seq) == 0
    zbuf_ref[0:SUBLANES, :] = jnp.where(seq_start, 0.0, carry_ref[j])
    zbuf_ref[SUBLANES:, :] = z
    carry_ref[j] = z[tm - SUBLANES:, :]
    cw = cw_ref[...]
    zc = (cw[2:3, :] * z
          + cw[1:2, :] * zbuf_ref[SUBLANES - 1:SUBLANES - 1 + tm, :]
          + cw[0:1, :] * zbuf_ref[SUBLANES - 2:SUBLANES - 2 + tm, :])
    bg = _dot(h_ref[tile_rows, :], wb_ref[...])
    acc_ref[tile_rows, :] += _dot((bg * zc).astype(BF16), wo_ref[...])


def _mixer(x, shift, scale, gate, g_pre, g_post, conv_w, w_in, w_out, casts, seq,
           tm=512, tn=512):
    d = x.shape[1]
    nj = d // tn

    def weight_specs(wj):
        return [
            pl.BlockSpec((3, tn), lambda i, j: (0, wj(i, j))),
            pl.BlockSpec((d, tn), lambda i, j: (0, wj(i, j))),
            pl.BlockSpec((d, tn), lambda i, j: (0, nj + wj(i, j))),
            pl.BlockSpec((d, tn), lambda i, j: (0, 2 * nj + wj(i, j))),
            pl.BlockSpec((tn, d), lambda i, j: (wj(i, j), 0)),
        ]

    return _sublayer(
        x, shift, scale, gate, g_pre, g_post, [conv_w, w_in, w_in, w_in, w_out], weight_specs,
        functools.partial(_mixer_step, tiles_per_seq=seq // tm), casts,
        seq=seq, tm=tm, nj=nj,
        extra_scratch=[pltpu.VMEM((tm + SUBLANES, tn), F32),
                       pltpu.VMEM((nj, SUBLANES, tn), F32)],
        name="conv_mixer")


MLP_SUB = 512


def _mlp_step(tile, j, w_refs, h_ref, acc_ref, tile_rows, extra, side_work):
    wup_ref, wdn_ref = w_refs
    for c in range(0, wup_ref.shape[1], MLP_SUB):
        if c == MLP_SUB:
            side_work()
        a = jnp.maximum(_dot(h_ref[tile_rows, :], wup_ref[:, c:c + MLP_SUB]), 0.0)
        acc_ref[tile_rows, :] += _dot((a * a).astype(BF16), wdn_ref[c:c + MLP_SUB, :])


def _mlp(x, shift, scale, gate, g_pre, g_post, w_up, w_down, casts, seq, tm=1024, tf=1024):
    d = x.shape[1]
    dff = w_up.shape[1]

    def weight_specs(wj):
        return [
            pl.BlockSpec((d, tf), lambda i, j: (0, wj(i, j))),
            pl.BlockSpec((tf, d), lambda i, j: (wj(i, j), 0)),
        ]

    return _sublayer(
        x, shift, scale, gate, g_pre, g_post, [w_up, w_down], weight_specs, _mlp_step, casts,
        seq=seq, tm=tm, nj=dff // tf, extra_scratch=[], name="relu2_mlp")


def _rope(t, cos_ref, s1_ref, s2_ref):
    n = t.shape[1]
    reps = n // LANES
    cos = jnp.tile(cos_ref[...], (1, reps))
    s1 = jnp.tile(s1_ref[...], (1, reps))
    s2 = jnp.tile(s2_ref[...], (1, reps))
    half = ROT_DIM // 2
    return (t * cos + pltpu.roll(t, n - half, 1) * s1 + pltpu.roll(t, half, 1) * s2)


def _dup_heads(t):
    lo = lax.broadcasted_iota(jnp.int32, (t.shape[0], LANES), 1) < HEAD_DIM
    out = []
    for m in range(t.shape[1] // LANES):
        slab = t[:, m * LANES:(m + 1) * LANES]
        swapped = pltpu.roll(slab, HEAD_DIM, 1)
        out += [jnp.where(lo, slab, swapped), jnp.where(lo, swapped, slab)]
    return jnp.concatenate(out, axis=1)


def _qkv_kernel(x_ref, shq_ref, scq_ref, gq_ref, shk_ref, sck_ref, gk_ref,
                wq_ref, bq_ref, wkv_ref, bkv_ref,
                cos_ref, s1_ref, s2_ref, q_ref, k_ref, v_ref):
    x = x_ref[...]
    xn = x * _rms_scale(x)
    hq = (xn * (gq_ref[...] * (1.0 + scq_ref[...])) + shq_ref[...]).astype(BF16)
    hk = (xn * (gk_ref[...] * (1.0 + sck_ref[...])) + shk_ref[...]).astype(BF16)
    q = _dot(hq, wq_ref[...]) + bq_ref[...]
    q_ref[...] = (_rope(q, cos_ref, s1_ref, s2_ref) * (HEAD_DIM ** -0.5)).astype(BF16)
    kv = _dot(hk, wkv_ref[...]) + bkv_ref[...]
    nk = kv.shape[1] // 2
    k_ref[...] = _dup_heads(_rope(kv[:, :nk], cos_ref, s1_ref, s2_ref)).astype(BF16)
    v_ref[...] = _dup_heads(kv[:, nk:]).astype(BF16)


def _qkv(x, shq, scq, gq, shk, sck, gk, wq, bq, wkv, bkv, cos, s1, s2, seq, tm=512):
    t, d = x.shape
    nq = wq.shape[1]
    nk = wkv.shape[1]
    tps = seq // tm
    mod_spec = pl.BlockSpec((None, 1, d), lambda i: (i // tps, 0, 0))
    vec_spec = pl.BlockSpec((1, d), lambda i: (0, 0))
    tab_spec = pl.BlockSpec((tm, LANES), lambda i: (i % tps, 0))

    def full(a):
        return pl.BlockSpec(a.shape, lambda i: (0, 0))

    return pl.pallas_call(
        _qkv_kernel,
        grid=(t // tm,),
        in_specs=[
            pl.BlockSpec((tm, d), lambda i: (i, 0)),
            mod_spec, mod_spec, vec_spec, mod_spec, mod_spec, vec_spec,
            full(wq), full(bq), full(wkv), full(bkv),
            tab_spec, tab_spec, tab_spec,
        ],
        out_specs=[
            pl.BlockSpec((tm, nq), lambda i: (i, 0)),
            pl.BlockSpec((tm, nk), lambda i: (i, 0)),
            pl.BlockSpec((tm, nk), lambda i: (i, 0)),
        ],
        out_shape=[
            jax.ShapeDtypeStruct((t, nq), BF16),
            jax.ShapeDtypeStruct((t, nk), BF16),
            jax.ShapeDtypeStruct((t, nk), BF16),
        ],
        compiler_params=_params(("parallel",)),
        name="qkv_proj",
    )(x, shq, scq, gq, shk, sck, gk, wq, bq, wkv, bkv, cos, s1, s2)


def _attn_kernel(sinks_ref, q_ref, kp_ref, kc_ref, vp_ref, vc_ref, o_ref, *, q_blocks):
    n = pl.program_id(1)
    blk = ATT_BLOCK
    n_kv = kp_ref.shape[1] // LANES
    pairs_per_kv = KV_GROUP // 2

    qi = lax.broadcasted_iota(jnp.int32, (2 * blk, LANES), 0) & (blk - 1)
    kc = lax.broadcasted_iota(jnp.int32, (2 * blk, LANES), 1)
    from_prev = kc > qi
    lo = lax.broadcasted_iota(jnp.int32, (blk, LANES), 1) < HEAD_DIM
    top = lax.broadcasted_iota(jnp.int32, (2 * blk, 1), 0) < blk
    ones = jnp.ones((blk, LANES), BF16)
    no_prev_bias = jnp.where(n == 0, -jnp.inf, 0.0)

    for g in range(n_kv):
        cols = slice(g * LANES, (g + 1) * LANES)
        for b in range(q_blocks):
            cur = slice(b * blk, (b + 1) * blk)
            if b == 0:
                k_prev, v_prev = kp_ref[:, cols], vp_ref[:, cols]
            else:
                prv = slice((b - 1) * blk, b * blk)
                k_prev, v_prev = kc_ref[prv, cols], vc_ref[prv, cols]
            kband = jnp.concatenate([k_prev, kc_ref[cur, cols]], axis=0)
            vext = jnp.concatenate(
                [jnp.concatenate([v_prev, ones], axis=1),
                 jnp.concatenate([vc_ref[cur, cols], ones], axis=1)], axis=0)
            for pp in range(pairs_per_kv):
                p = g * pairs_per_kv + pp
                pcols = slice(p * LANES, (p + 1) * LANES)
                qp = q_ref[cur, pcols]
                zero = jnp.zeros_like(qp)
                q2 = jnp.concatenate([jnp.where(lo, qp, zero), jnp.where(lo, zero, qp)], axis=0)
                s = lax.dot_general(q2, kband, (((1,), (1,)), ((), ())),
                                    preferred_element_type=F32)
                s_prev = s[:, :LANES]
                if b == 0:
                    s_prev = s_prev + no_prev_bias
                sc = jnp.where(from_prev, s_prev, s[:, LANES:])
                sink = jnp.where(top, sinks_ref[2 * p], sinks_ref[2 * p + 1])
                m = jnp.maximum(jnp.max(sc, axis=-1, keepdims=True), sink)
                e = jnp.exp(sc - m).astype(BF16)
                ez = jnp.zeros_like(e)
                pmat = jnp.concatenate([jnp.where(from_prev, e, ez),
                                        jnp.where(from_prev, ez, e)], axis=1)
                pv = _dot(pmat, vext)
                den = pv[:, LANES:] + jnp.exp(sink - m)
                num = jnp.where(lo, pv[:blk, :LANES], pv[blk:, :LANES])
                o_ref[cur, pcols] = (num / jnp.where(lo, den[:blk], den[blk:])).astype(BF16)


def _attention(sinks, q, kd, vd, seq, q_blocks=2):
    t, nq = q.shape
    nk = kd.shape[1]
    tq = q_blocks * ATT_BLOCK
    steps = seq // tq
    batch = t // seq
    cur = lambda b, n: (b * steps + n, 0)
    prev = lambda b, n: ((b * steps + n) * q_blocks - jnp.where(n == 0, 0, 1), 0)
    return pl.pallas_call(
        functools.partial(_attn_kernel, q_blocks=q_blocks),
        grid=(batch, steps),
        in_specs=[
            pl.BlockSpec(memory_space=pltpu.SMEM),
            pl.BlockSpec((tq, nq), cur),
            pl.BlockSpec((ATT_BLOCK, nk), prev),
            pl.BlockSpec((tq, nk), cur),
            pl.BlockSpec((ATT_BLOCK, nk), prev),
            pl.BlockSpec((tq, nk), cur),
        ],
        out_specs=pl.BlockSpec((tq, nq), cur),
        out_shape=jax.ShapeDtypeStruct((t, nq), BF16),
        compiler_params=_params(("parallel", "parallel")),
        name="swa_attention",
    )(sinks, q, kd, kd, vd, vd)


def _oproj_kernel(x_ref, o_ref, gt_ref, gpost_ref, wo_ref, bo_ref, out_ref):
    y = _dot(o_ref[...], wo_ref[...]) + bo_ref[...]
    out_ref[...] = x_ref[...] + gt_ref[...] * (y * _rms_scale(y) * gpost_ref[...])


def _oproj(x, o, gate, g_post, w_o, b_o, seq, tm=512):
    t, d = x.shape
    nq = o.shape[1]
    tps = seq // tm
    return pl.pallas_call(
        _oproj_kernel,
        grid=(t // tm,),
        in_specs=[
            pl.BlockSpec((tm, d), lambda i: (i, 0)),
            pl.BlockSpec((tm, nq), lambda i: (i, 0)),
            pl.BlockSpec((None, 1, d), lambda i: (i // tps, 0, 0)),
            pl.BlockSpec((1, d), lambda i: (0, 0)),
            pl.BlockSpec((nq, d), lambda i: (0, 0)),
            pl.BlockSpec((1, d), lambda i: (0, 0)),
        ],
        out_specs=pl.BlockSpec((tm, d), lambda i: (i, 0)),
        out_shape=jax.ShapeDtypeStruct((t, d), F32),
        compiler_params=_params(("parallel",)),
        name="attn_out_proj",
    )(x, o, gate, g_post, w_o, b_o)


def _rope_tables(seq):
    half = ROT_DIM // 2
    inv = ROPE_THETA ** (-jnp.arange(0, ROT_DIM, 2, dtype=F32) / ROT_DIM)
    ang = jnp.arange(seq, dtype=F32)[:, None] * inv[None, :]
    cos, sin = jnp.cos(ang), jnp.sin(ang)
    ones = jnp.ones((seq, HEAD_DIM - ROT_DIM), F32)
    zeros = jnp.zeros((seq, HEAD_DIM - ROT_DIM), F32)
    zh = jnp.zeros((seq, half), F32)
    c_tab = jnp.concatenate([cos, cos, ones], axis=1)
    s1_tab = jnp.concatenate([-sin, zh, zeros], axis=1)
    s2_tab = jnp.concatenate([zh, sin, zeros], axis=1)
    rep = LANES // HEAD_DIM
    return tuple(jnp.tile(a, (1, rep)) for a in (c_tab, s1_tab, s2_tab))


def kernel(x, c, ada_w, ada_b, norm_pre, norm_post, conv_w_in, conv_w, conv_w_out,
           kv_ada_w, kv_ada_b, kv_norm, w_kv, b_kv, w_q, b_q, sinks, w_o, b_o,
           mlp_up, mlp_down):
    batch, seq, d = x.shape
    depth = ada_w.shape[0]
    n_self = conv_w_in.shape[0]
    t = batch * seq
    assert n_self == 1 and depth == 2, "one conv-mixer layer then one shared-KV attention layer"

    c_pad = jnp.pad(c, ((0, SUBLANES - batch), (0, 0)))
    mods = _ada(c_pad, ada_w.reshape(depth * 2, d, 3 * d), ada_b.reshape(depth * 2, 1, 3 * d))
    mods = mods[:, :batch].reshape(depth, 2, batch, 3, 1, d)
    kv_mods = _ada(c_pad, kv_ada_w[None], kv_ada_b[None, None])
    kv_mods = kv_mods[0, :batch].reshape(batch, 2, 1, d)

    def mod(l, sub):
        return [mods[l, sub, :, which] for which in range(3)]

    xf = x.reshape(t, d)

    xf, (up0, down0) = _mixer(
        xf, *mod(0, 0), norm_pre[0, 0:1], norm_post[0, 0:1], conv_w[0],
        conv_w_in[0].astype(BF16), conv_w_out[0].astype(BF16),
        [(mlp_up, 0), (mlp_down, 0)], seq)
    xf, (wq_bf, wo_bf, up1, down1) = _mlp(
        xf, *mod(0, 1), norm_pre[0, 1:2], norm_post[0, 1:2], up0, down0,
        [(w_q, 0), (w_o, 0), (mlp_up, 1), (mlp_down, 1)], seq)

    cos, s1, s2 = _rope_tables(seq)
    shift, scale, gate = mod(1, 0)
    q, kd, vd = _qkv(xf, shift, scale, norm_pre[1, 0:1],
                     kv_mods[:, 0], kv_mods[:, 1], kv_norm[None],
                     wq_bf, b_q[0][None], w_kv.astype(BF16), b_kv[None], cos, s1, s2, seq)
    o = _attention(sinks[0], q, kd, vd, seq)
    xf = _oproj(xf, o, gate, norm_post[1, 0:1], wo_bf, b_o[0][None], seq)
    xf, _ = _mlp(xf, *mod(1, 1), norm_pre[1, 1:2], norm_post[1, 1:2], up1, down1, [], seq)
    return xf.reshape(batch, seq, d)
```

```python
import functools

import jax
import jax.numpy as jnp
from jax import lax
from jax.experimental import pallas as pl
from jax.experimental.pallas import tpu as pltpu

EPS = 1e-6
HEAD_DIM = 64
KV_GROUP = 8
ROT_DIM = HEAD_DIM // 4
ROPE_THETA = 500000.0
ATT_BLOCK = 128
LANES = 128
SUBLANES = 8
VMEM_LIMIT = 56 * 1024 * 1024
ROW_SUB = 32

BF16 = jnp.bfloat16
F32 = jnp.float32


def _dot(a, b):
    return jnp.dot(a, b, preferred_element_type=F32)


def _rms_scale(x):
    return lax.rsqrt(jnp.mean(x * x, axis=-1, keepdims=True) + EPS)


def _params(sem):
    return pltpu.CompilerParams(dimension_semantics=sem, vmem_limit_bytes=VMEM_LIMIT)


def _ada_kernel(c_ref, w_ref, b_ref, o_ref):
    c = c_ref[...]
    c_act = c / (1.0 + jnp.exp(-c))
    o_ref[...] = _dot(c_act.astype(BF16), w_ref[...].astype(BF16)) + b_ref[...]


def _ada(c_pad, w, b, tn=512):
    g, d, n = w.shape
    return pl.pallas_call(
        _ada_kernel,
        grid=(g, n // tn),
        in_specs=[
            pl.BlockSpec((SUBLANES, d), lambda a, j: (0, 0)),
            pl.BlockSpec((None, d, tn), lambda a, j: (a, 0, j)),
            pl.BlockSpec((None, 1, tn), lambda a, j: (a, 0, j)),
        ],
        out_specs=pl.BlockSpec((None, SUBLANES, tn), lambda a, j: (a, 0, j)),
        out_shape=jax.ShapeDtypeStruct((g, SUBLANES, n), F32),
        compiler_params=_params(("parallel", "parallel")),
        name="ada_mod",
    )(c_pad, w, b)


def _sublayer_kernel(*refs, n_tiles, row_split, n_w, n_cast, step_fn):
    xn_ref, xr_ref, sh_ref, sc_ref, gt_ref, gpre_ref, gpost_ref = refs[:7]
    k = 7
    w_refs = refs[k:k + n_w]
    k += n_w
    cast_in = refs[k:k + n_cast]
    k += n_cast
    out_ref = refs[k]
    cast_out = refs[k + 1:k + 1 + n_cast]
    k += 1 + n_cast
    h_ref, acc_ref = refs[k:k + 2]
    extra = refs[k + 2:]

    i = pl.program_id(0)
    j = pl.program_id(1)
    rc, d = xn_ref.shape
    tm = h_ref.shape[0] // 2
    row0 = j * rc

    def tile_base(t):
        if isinstance(t, int):
            return (t % 2) * tm
        return pl.multiple_of((t % 2) * tm, tm)

    def rows(base, s, size):
        return pl.ds(pl.multiple_of(base + row0 + s, size), size)

    def normalise(base):
        a = gpre_ref[...] * (1.0 + sc_ref[...])
        sh = sh_ref[...]
        for s in range(0, rc, ROW_SUB):
            x = xn_ref[s:s + ROW_SUB, :]
            h_ref[rows(base, s, ROW_SUB), :] = (x * _rms_scale(x) * a + sh).astype(BF16)

    def finish(base):
        gg = gt_ref[...] * gpost_ref[...]
        for s in range(0, rc, ROW_SUB):
            y = acc_ref[rows(base, s, ROW_SUB), :]
            out_ref[s:s + ROW_SUB, :] = xr_ref[s:s + ROW_SUB, :] + y * _rms_scale(y) * gg

    def clear(base):
        acc_ref[rows(base, 0, rc), :] = jnp.zeros((rc, d), F32)

    @pl.when(i == 0)
    def _():
        normalise(tile_base(0))
        clear(tile_base(0))
        clear(tile_base(1))

    @pl.when((i >= 1) & (i <= n_tiles))
    def _():
        def side_work():
            finish(tile_base(i))
            clear(tile_base(i))
            normalise(tile_base(i))
            for src, dst in zip(cast_in, cast_out):
                dst[...] = src[...].astype(BF16)

        part = tm // row_split
        part_rows = pl.ds(pl.multiple_of(tile_base(i - 1) + (j % row_split) * part, part), part)
        step_fn(i - 1, j // row_split, j % row_split, w_refs, h_ref, acc_ref, part_rows, extra,
                side_work)

    @pl.when(i == n_tiles + 1)
    def _():
        finish(tile_base(n_tiles - 1))


def _sublayer(x, shift, scale, gate, g_pre, g_post, weights, weight_specs, step_fn, casts,
              *, seq, tm, n_blocks, row_split, extra_scratch, name):
    t, d = x.shape
    n = t // tm
    nj = n_blocks * row_split
    rc = tm // nj
    tps = seq // tm
    steps = n * nj

    def norm_tile(i):
        return jnp.minimum(i, n - 1)

    def done_tile(i):
        return jnp.clip(i - 2, 0, n - 1)

    def done_map(i, j):
        return (done_tile(i) * nj + jnp.where(i < 2, 0, j), 0)

    def wj(i, j):
        return jnp.where(i == 0, 0, jnp.where(i == n + 1, n_blocks - 1, j // row_split))

    def cast_step(i, j):
        return jnp.clip((i - 1) * nj + j, 0, steps - 1)

    pre_mod = pl.BlockSpec((None, 1, d), lambda i, j: (norm_tile(i) // tps, 0, 0))
    post_mod = pl.BlockSpec((None, 1, d), lambda i, j: (done_tile(i) // tps, 0, 0))
    vec = pl.BlockSpec((1, d), lambda i, j: (0, 0))
    cast_in_specs, cast_out_specs, cast_shapes = [], [], []
    for w, layer in casts:
        _, r, c = w.shape
        assert r % (steps * 2 * SUBLANES) == 0, "cast row blocks must be whole bf16 tiles"
        cast_in_specs.append(pl.BlockSpec(
            (None, r // steps, c), lambda i, j, layer=layer: (layer, cast_step(i, j), 0)))
        cast_out_specs.append(pl.BlockSpec((r // steps, c), lambda i, j: (cast_step(i, j), 0)))
        cast_shapes.append(jax.ShapeDtypeStruct((r, c), BF16))

    outs = pl.pallas_call(
        functools.partial(_sublayer_kernel, n_tiles=n, row_split=row_split, n_w=len(weights),
                          n_cast=len(casts), step_fn=step_fn),
        grid=(n + 2, nj),
        in_specs=[
            pl.BlockSpec((rc, d), lambda i, j: (norm_tile(i) * nj + j, 0)),
            pl.BlockSpec((rc, d), done_map),
            pre_mod, pre_mod, post_mod, vec, vec,
            *weight_specs(wj),
            *cast_in_specs,
        ],
        out_specs=[pl.BlockSpec((rc, d), done_map), *cast_out_specs],
        out_shape=[jax.ShapeDtypeStruct((t, d), F32), *cast_shapes],
        scratch_shapes=[
            pltpu.VMEM((2 * tm, d), BF16),
            pltpu.VMEM((2 * tm, d), F32),
            *extra_scratch,
        ],
        compiler_params=_params(("arbitrary", "arbitrary")),
        name=name,
    )(x, x, shift, scale, gate, g_pre, g_post, *weights, *[w for w, _ in casts])
    return outs[0], outs[1:]


def _mixer_step(tile, blk, part, w_refs, h_ref, acc_ref, tile_rows, extra, side_work, *,
                tiles_per_seq):
    cw_ref, wb_ref, wc_ref, wu_ref, wo_ref = w_refs
    zbuf_ref, carry_ref = extra
    tm = tile_rows.size
    cg = _dot(h_ref[tile_rows, :], wc_ref[...])
    side_work()
    h = h_ref[tile_rows, :]
    z = cg * _dot(h, wu_ref[...])

    seq_start = ((tile % tiles_per_seq) == 0) & (part == 0)
    zbuf_ref[0:SUBLANES, :] = jnp.where(seq_start, 0.0, carry_ref[blk])
    zbuf_ref[SUBLANES:, :] = z
    carry_ref[blk] = z[tm - SUBLANES:, :]
    cw = cw_ref[...]
    zc = (cw[2:3, :] * z
          + cw[1:2, :] * zbuf_ref[SUBLANES - 1:SUBLANES - 1 + tm, :]
          + cw[0:1, :] * zbuf_ref[SUBLANES - 2:SUBLANES - 2 + tm, :])
    bg = _dot(h_ref[tile_rows, :], wb_ref[...])
    acc_ref[tile_rows, :] += _dot((bg * zc).astype(BF16), wo_ref[...])


def _mixer(x, shift, scale, gate, g_pre, g_post, conv_w, w_in, w_out, casts, seq,
           tm=1024, tn=512, row_split=2):
    d = x.shape[1]
    nj = d // tn

    def weight_specs(wj):
        return [
            pl.BlockSpec((3, tn), lambda i, j: (0, wj(i, j))),
            pl.BlockSpec((d, tn), lambda i, j: (0, wj(i, j))),
            pl.BlockSpec((d, tn), lambda i, j: (0, nj + wj(i, j))),
            pl.BlockSpec((d, tn), lambda i, j: (0, 2 * nj + wj(i, j))),
            pl.BlockSpec((tn, d), lambda i, j: (wj(i, j), 0)),
        ]

    return _sublayer(
        x, shift, scale, gate, g_pre, g_post, [conv_w, w_in, w_in, w_in, w_out], weight_specs,
        functools.partial(_mixer_step, tiles_per_seq=seq // tm), casts,
        seq=seq, tm=tm, n_blocks=nj, row_split=row_split,
        extra_scratch=[pltpu.VMEM((tm // row_split + SUBLANES, tn), F32),
                       pltpu.VMEM((nj, SUBLANES, tn), F32)],
        name="conv_mixer")


MLP_SUB = 512


def _mlp_step(tile, blk, part, w_refs, h_ref, acc_ref, tile_rows, extra, side_work):
    wup_ref, wdn_ref = w_refs
    for c in range(0, wup_ref.shape[1], MLP_SUB):
        if c == MLP_SUB:
            side_work()
        a = jnp.maximum(_dot(h_ref[tile_rows, :], wup_ref[:, c:c + MLP_SUB]), 0.0)
        acc_ref[tile_rows, :] += _dot((a * a).astype(BF16), wdn_ref[c:c + MLP_SUB, :])


def _mlp(x, shift, scale, gate, g_pre, g_post, w_up, w_down, casts, seq, tm=1024, tf=1024):
    d = x.shape[1]
    dff = w_up.shape[1]

    def weight_specs(wj):
        return [
            pl.BlockSpec((d, tf), lambda i, j: (0, wj(i, j))),
            pl.BlockSpec((tf, d), lambda i, j: (wj(i, j), 0)),
        ]

    return _sublayer(
        x, shift, scale, gate, g_pre, g_post, [w_up, w_down], weight_specs, _mlp_step, casts,
        seq=seq, tm=tm, n_blocks=dff // tf, row_split=1, extra_scratch=[], name="relu2_mlp")


def _rope(t, cos_ref, s1_ref, s2_ref):
    n = t.shape[1]
    reps = n // LANES
    cos = jnp.tile(cos_ref[...], (1, reps))
    s1 = jnp.tile(s1_ref[...], (1, reps))
    s2 = jnp.tile(s2_ref[...], (1, reps))
    half = ROT_DIM // 2
    return (t * cos + pltpu.roll(t, n - half, 1) * s1 + pltpu.roll(t, half, 1) * s2)


def _dup_heads(t):
    lo = lax.broadcasted_iota(jnp.int32, (t.shape[0], LANES), 1) < HEAD_DIM
    out = []
    for m in range(t.shape[1] // LANES):
        slab = t[:, m * LANES:(m + 1) * LANES]
        swapped = pltpu.roll(slab, HEAD_DIM, 1)
        out += [jnp.where(lo, slab, swapped), jnp.where(lo, swapped, slab)]
    return jnp.concatenate(out, axis=1)


def _qkv_kernel(x_ref, shq_ref, scq_ref, gq_ref, shk_ref, sck_ref, gk_ref,
                wq_ref, bq_ref, wkv_ref, bkv_ref,
                cos_ref, s1_ref, s2_ref, q_ref, k_ref, v_ref):
    x = x_ref[...]
    xn = x * _rms_scale(x)
    hq = (xn * (gq_ref[...] * (1.0 + scq_ref[...])) + shq_ref[...]).astype(BF16)
    hk = (xn * (gk_ref[...] * (1.0 + sck_ref[...])) + shk_ref[...]).astype(BF16)
    q = _dot(hq, wq_ref[...]) + bq_ref[...]
    q_ref[...] = (_rope(q, cos_ref, s1_ref, s2_ref) * (HEAD_DIM ** -0.5)).astype(BF16)
    kv = _dot(hk, wkv_ref[...]) + bkv_ref[...]
    nk = kv.shape[1] // 2
    k_ref[...] = _dup_heads(_rope(kv[:, :nk], cos_ref, s1_ref, s2_ref)).astype(BF16)
    v_ref[...] = _dup_heads(kv[:, nk:]).astype(BF16)


def _qkv(x, shq, scq, gq, shk, sck, gk, wq, bq, wkv, bkv, cos, s1, s2, seq, tm=512):
    t, d = x.shape
    nq = wq.shape[1]
    nk = wkv.shape[1]
    tps = seq // tm
    mod_spec = pl.BlockSpec((None, 1, d), lambda i: (i // tps, 0, 0))
    vec_spec = pl.BlockSpec((1, d), lambda i: (0, 0))
    tab_spec = pl.BlockSpec((tm, LANES), lambda i: (i % tps, 0))

    def full(a):
        return pl.BlockSpec(a.shape, lambda i: (0, 0))

    return pl.pallas_call(
        _qkv_kernel,
        grid=(t // tm,),
        in_specs=[
            pl.BlockSpec((tm, d), lambda i: (i, 0)),
            mod_spec, mod_spec, vec_spec, mod_spec, mod_spec, vec_spec,
            full(wq), full(bq), full(wkv), full(bkv),
            tab_spec, tab_spec, tab_spec,
        ],
        out_specs=[
            pl.BlockSpec((tm, nq), lambda i: (i, 0)),
            pl.BlockSpec((tm, nk), lambda i: (i, 0)),
            pl.BlockSpec((tm, nk), lambda i: (i, 0)),
        ],
        out_shape=[
            jax.ShapeDtypeStruct((t, nq), BF16),
            jax.ShapeDtypeStruct((t, nk), BF16),
            jax.ShapeDtypeStruct((t, nk), BF16),
        ],
        compiler_params=_params(("parallel",)),
        name="qkv_proj",
    )(x, shq, scq, gq, shk, sck, gk, wq, bq, wkv, bkv, cos, s1, s2)


def _attn_kernel(sinks_ref, q_ref, kp_ref, kc_ref, vp_ref, vc_ref, o_ref, *, q_blocks):
    n = pl.program_id(1)
    blk = ATT_BLOCK
    n_kv = kp_ref.shape[1] // LANES
    pairs_per_kv = KV_GROUP // 2

    qi = lax.broadcasted_iota(jnp.int32, (2 * blk, LANES), 0) & (blk - 1)
    kc = lax.broadcasted_iota(jnp.int32, (2 * blk, LANES), 1)
    from_prev = kc > qi
    lo = lax.broadcasted_iota(jnp.int32, (blk, LANES), 1) < HEAD_DIM
    top = lax.broadcasted_iota(jnp.int32, (2 * blk, 1), 0) < blk
    ones = jnp.ones((blk, LANES), BF16)
    no_prev_bias = jnp.where(n == 0, -jnp.inf, 0.0)

    for g in range(n_kv):
        cols = slice(g * LANES, (g + 1) * LANES)
        for b in range(q_blocks):
            cur = slice(b * blk, (b + 1) * blk)
            if b == 0:
                k_prev, v_prev = kp_ref[:, cols], vp_ref[:, cols]
            else:
                prv = slice((b - 1) * blk, b * blk)
                k_prev, v_prev = kc_ref[prv, cols], vc_ref[prv, cols]
            kband = jnp.concatenate([k_prev, kc_ref[cur, cols]], axis=0)
            vext = jnp.concatenate(
                [jnp.concatenate([v_prev, ones], axis=1),
                 jnp.concatenate([vc_ref[cur, cols], ones], axis=1)], axis=0)
            for pp in range(pairs_per_kv):
                p = g * pairs_per_kv + pp
                pcols = slice(p * LANES, (p + 1) * LANES)
                qp = q_ref[cur, pcols]
                zero = jnp.zeros_like(qp)
                q2 = jnp.concatenate([jnp.where(lo, qp, zero), jnp.where(lo, zero, qp)], axis=0)
                s = lax.dot_general(q2, kband, (((1,), (1,)), ((), ())),
                                    preferred_element_type=F32)
                s_prev = s[:, :LANES]
                if b == 0:
                    s_prev = s_prev + no_prev_bias
                sc = jnp.where(from_prev, s_prev, s[:, LANES:])
                sink = jnp.where(top, sinks_ref[2 * p], sinks_ref[2 * p + 1])
                m = jnp.maximum(jnp.max(sc, axis=-1, keepdims=True), sink)
                e = jnp.exp(sc - m).astype(BF16)
                ez = jnp.zeros_like(e)
                pmat = jnp.concatenate([jnp.where(from_prev, e, ez),
                                        jnp.where(from_prev, ez, e)], axis=1)
                pv = _dot(pmat, vext)
                den = pv[:, LANES:] + jnp.exp(sink - m)
                num = jnp.where(lo, pv[:blk, :LANES], pv[blk:, :LANES])
                o_ref[cur, pcols] = (num / jnp.where(lo, den[:blk], den[blk:])).astype(BF16)


def _attention(sinks, q, kd, vd, seq, q_blocks=2):
    t, nq = q.shape
    nk = kd.shape[1]
    tq = q_blocks * ATT_BLOCK
    steps = seq // tq
    batch = t // seq
    cur = lambda b, n: (b * steps + n, 0)
    prev = lambda b, n: ((b * steps + n) * q_blocks - jnp.where(n == 0, 0, 1), 0)
    return pl.pallas_call(
        functools.partial(_attn_kernel, q_blocks=q_blocks),
        grid=(batch, steps),
        in_specs=[
            pl.BlockSpec(memory_space=pltpu.SMEM),
            pl.BlockSpec((tq, nq), cur),
            pl.BlockSpec((ATT_BLOCK, nk), prev),
            pl.BlockSpec((tq, nk), cur),
            pl.BlockSpec((ATT_BLOCK, nk), prev),
            pl.BlockSpec((tq, nk), cur),
        ],
        out_specs=pl.BlockSpec((tq, nq), cur),
        out_shape=jax.ShapeDtypeStruct((t, nq), BF16),
        compiler_params=_params(("parallel", "parallel")),
        name="swa_attention",
    )(sinks, q, kd, kd, vd, vd)


def _oproj_kernel(x_ref, o_ref, gt_ref, gpost_ref, wo_ref, bo_ref, out_ref):
    y = _dot(o_ref[...], wo_ref[...]) + bo_ref[...]
    out_ref[...] = x_ref[...] + gt_ref[...] * (y * _rms_scale(y) * gpost_ref[...])


def _oproj(x, o, gate, g_post, w_o, b_o, seq, tm=512):
    t, d = x.shape
    nq = o.shape[1]
    tps = seq // tm
    return pl.pallas_call(
        _oproj_kernel,
        grid=(t // tm,),
        in_specs=[
            pl.BlockSpec((tm, d), lambda i: (i, 0)),
            pl.BlockSpec((tm, nq), lambda i: (i, 0)),
            pl.BlockSpec((None, 1, d), lambda i: (i // tps, 0, 0)),
            pl.BlockSpec((1, d), lambda i: (0, 0)),
            pl.BlockSpec((nq, d), lambda i: (0, 0)),
            pl.BlockSpec((1, d), lambda i: (0, 0)),
        ],
        out_specs=pl.BlockSpec((tm, d), lambda i: (i, 0)),
        out_shape=jax.ShapeDtypeStruct((t, d), F32),
        compiler_params=_params(("parallel",)),
        name="attn_out_proj",
    )(x, o, gate, g_post, w_o, b_o)


def _rope_tables(seq):
    half = ROT_DIM // 2
    inv = ROPE_THETA ** (-jnp.arange(0, ROT_DIM, 2, dtype=F32) / ROT_DIM)
    ang = jnp.arange(seq, dtype=F32)[:, None] * inv[None, :]
    cos, sin = jnp.cos(ang), jnp.sin(ang)
    ones = jnp.ones((seq, HEAD_DIM - ROT_DIM), F32)
    zeros = jnp.zeros((seq, HEAD_DIM - ROT_DIM), F32)
    zh = jnp.zeros((seq, half), F32)
    c_tab = jnp.concatenate([cos, cos, ones], axis=1)
    s1_tab = jnp.concatenate([-sin, zh, zeros], axis=1)
    s2_tab = jnp.concatenate([zh, sin, zeros], axis=1)
    rep = LANES // HEAD_DIM
    return tuple(jnp.tile(a, (1, rep)) for a in (c_tab, s1_tab, s2_tab))


def kernel(x, c, ada_w, ada_b, norm_pre, norm_post, conv_w_in, conv_w, conv_w_out,
           kv_ada_w, kv_ada_b, kv_norm, w_kv, b_kv, w_q, b_q, sinks, w_o, b_o,
           mlp_up, mlp_down):
    batch, seq, d = x.shape
    depth = ada_w.shape[0]
    n_self = conv_w_in.shape[0]
    t = batch * seq
    assert n_self == 1 and depth == 2, "one conv-mixer layer then one shared-KV attention layer"

    c_pad = jnp.pad(c, ((0, SUBLANES - batch), (0, 0)))
    mods = _ada(c_pad, ada_w.reshape(depth * 2, d, 3 * d), ada_b.reshape(depth * 2, 1, 3 * d))
    mods = mods[:, :batch].reshape(depth, 2, batch, 3, 1, d)
    kv_mods = _ada(c_pad, kv_ada_w[None], kv_ada_b[None, None])
    kv_mods = kv_mods[0, :batch].reshape(batch, 2, 1, d)

    def mod(l, sub):
        return [mods[l, sub, :, which] for which in range(3)]

    xf = x.reshape(t, d)

    xf, (up0, down0) = _mixer(
        xf, *mod(0, 0), norm_pre[0, 0:1], norm_post[0, 0:1], conv_w[0],
        conv_w_in[0].astype(BF16), conv_w_out[0].astype(BF16),
        [(mlp_up, 0), (mlp_down, 0)], seq)
    xf, (wq_bf, wo_bf, up1, down1) = _mlp(
        xf, *mod(0, 1), norm_pre[0, 1:2], norm_post[0, 1:2], up0, down0,
        [(w_q, 0), (w_o, 0), (mlp_up, 1), (mlp_down, 1)], seq)

    cos, s1, s2 = _rope_tables(seq)
    shift, scale, gate = mod(1, 0)
    q, kd, vd = _qkv(xf, shift, scale, norm_pre[1, 0:1],
                     kv_mods[:, 0], kv_mods[:, 1], kv_norm[None],
                     wq_bf, b_q[0][None], w_kv.astype(BF16), b_kv[None], cos, s1, s2, seq)
    o = _attention(sinks[0], q, kd, vd, seq)
    xf = _oproj(xf, o, gate, norm_post[1, 0:1], wo_bf, b_o[0][None], seq)
    xf, _ = _mlp(xf, *mod(1, 1), norm_pre[1, 1:2], norm_post[1, 1:2], up1, down1, [], seq)
    return xf.reshape(batch, seq, d)
```

```python
import functools

import jax
import jax.numpy as jnp
from jax import lax
from jax.experimental import pallas as pl
from jax.experimental.pallas import tpu as pltpu

EPS = 1e-6
HEAD_DIM = 64
KV_GROUP = 8
ROT_DIM = HEAD_DIM // 4
ROPE_THETA = 500000.0
ATT_BLOCK = 128
LANES = 128
SUBLANES = 8
VMEM_LIMIT = 56 * 1024 * 1024
ROW_SUB = 32

BF16 = jnp.bfloat16
F32 = jnp.float32


def _dot(a, b):
    return jnp.dot(a, b, preferred_element_type=F32)


def _rms_scale(x):
    return lax.rsqrt(jnp.mean(x * x, axis=-1, keepdims=True) + EPS)


def _params(sem):
    return pltpu.CompilerParams(dimension_semantics=sem, vmem_limit_bytes=VMEM_LIMIT)


def _ada_kernel(c_ref, w_ref, b_ref, o_ref):
    c = c_ref[...]
    c_act = c / (1.0 + jnp.exp(-c))
    o_ref[...] = _dot(c_act.astype(BF16), w_ref[...].astype(BF16)) + b_ref[...]


def _ada(c_pad, w, b, tn=512):
    g, d, n = w.shape
    return pl.pallas_call(
        _ada_kernel,
        grid=(g, n // tn),
        in_specs=[
            pl.BlockSpec((SUBLANES, d), lambda a, j: (0, 0)),
            pl.BlockSpec((None, d, tn), lambda a, j: (a, 0, j)),
            pl.BlockSpec((None, 1, tn), lambda a, j: (a, 0, j)),
        ],
        out_specs=pl.BlockSpec((None, SUBLANES, tn), lambda a, j: (a, 0, j)),
        out_shape=jax.ShapeDtypeStruct((g, SUBLANES, n), F32),
        compiler_params=_params(("parallel", "parallel")),
        name="ada_mod",
    )(c_pad, w, b)


def _sublayer_kernel(*refs, n_tiles, row_split, n_w, n_cast, step_fn):
    xn_ref, xr_ref, sh_ref, sc_ref, gt_ref, gpre_ref, gpost_ref = refs[:7]
    k = 7
    w_refs = refs[k:k + n_w]
    k += n_w
    cast_in = refs[k:k + n_cast]
    k += n_cast
    out_ref = refs[k]
    cast_out = refs[k + 1:k + 1 + n_cast]
    k += 1 + n_cast
    h_ref, acc_ref = refs[k:k + 2]
    extra = refs[k + 2:]

    i = pl.program_id(0)
    j = pl.program_id(1)
    rc, d = xn_ref.shape
    tm = h_ref.shape[0] // 2
    row0 = j * rc

    def tile_base(t):
        if isinstance(t, int):
            return (t % 2) * tm
        return pl.multiple_of((t % 2) * tm, tm)

    def rows(base, s, size):
        return pl.ds(pl.multiple_of(base + row0 + s, size), size)

    def normalise(base):
        a = gpre_ref[...] * (1.0 + sc_ref[...])
        sh = sh_ref[...]
        for s in range(0, rc, ROW_SUB):
            x = xn_ref[s:s + ROW_SUB, :]
            h_ref[rows(base, s, ROW_SUB), :] = (x * _rms_scale(x) * a + sh).astype(BF16)

    def finish(base):
        gg = gt_ref[...] * gpost_ref[...]
        for s in range(0, rc, ROW_SUB):
            y = acc_ref[rows(base, s, ROW_SUB), :]
            out_ref[s:s + ROW_SUB, :] = xr_ref[s:s + ROW_SUB, :] + y * _rms_scale(y) * gg

    def clear(base):
        acc_ref[rows(base, 0, rc), :] = jnp.zeros((rc, d), F32)

    @pl.when(i == 0)
    def _():
        normalise(tile_base(0))
        clear(tile_base(0))
        clear(tile_base(1))

    @pl.when((i >= 1) & (i <= n_tiles))
    def _():
        def side_finish():
            finish(tile_base(i))
            clear(tile_base(i))

        def side_norm():
            normalise(tile_base(i))
            for src, dst in zip(cast_in, cast_out):
                dst[...] = src[...].astype(BF16)

        part = tm // row_split
        part_rows = pl.ds(pl.multiple_of(tile_base(i - 1) + (j % row_split) * part, part), part)
        step_fn(i - 1, j // row_split, j % row_split, w_refs, h_ref, acc_ref, part_rows, extra,
                side_finish, side_norm)

    @pl.when(i == n_tiles + 1)
    def _():
        finish(tile_base(n_tiles - 1))


def _sublayer(x, shift, scale, gate, g_pre, g_post, weights, weight_specs, step_fn, casts,
              *, seq, tm, n_blocks, row_split, extra_scratch, name):
    t, d = x.shape
    n = t // tm
    nj = n_blocks * row_split
    rc = tm // nj
    tps = seq // tm
    steps = n * nj

    def norm_tile(i):
        return jnp.minimum(i, n - 1)

    def done_tile(i):
        return jnp.clip(i - 2, 0, n - 1)

    def done_map(i, j):
        return (done_tile(i) * nj + jnp.where(i < 2, 0, j), 0)

    def wj(i, j):
        return jnp.where(i == 0, 0, jnp.where(i == n + 1, n_blocks - 1, j // row_split))

    def cast_step(i, j):
        return jnp.clip((i - 1) * nj + j, 0, steps - 1)

    pre_mod = pl.BlockSpec((None, 1, d), lambda i, j: (norm_tile(i) // tps, 0, 0))
    post_mod = pl.BlockSpec((None, 1, d), lambda i, j: (done_tile(i) // tps, 0, 0))
    vec = pl.BlockSpec((1, d), lambda i, j: (0, 0))
    cast_in_specs, cast_out_specs, cast_shapes = [], [], []
    for w, layer in casts:
        _, r, c = w.shape
        assert r % (steps * 2 * SUBLANES) == 0, "cast row blocks must be whole bf16 tiles"
        cast_in_specs.append(pl.BlockSpec(
            (None, r // steps, c), lambda i, j, layer=layer: (layer, cast_step(i, j), 0)))
        cast_out_specs.append(pl.BlockSpec((r // steps, c), lambda i, j: (cast_step(i, j), 0)))
        cast_shapes.append(jax.ShapeDtypeStruct((r, c), BF16))

    outs = pl.pallas_call(
        functools.partial(_sublayer_kernel, n_tiles=n, row_split=row_split, n_w=len(weights),
                          n_cast=len(casts), step_fn=step_fn),
        grid=(n + 2, nj),
        in_specs=[
            pl.BlockSpec((rc, d), lambda i, j: (norm_tile(i) * nj + j, 0)),
            pl.BlockSpec((rc, d), done_map),
            pre_mod, pre_mod, post_mod, vec, vec,
            *weight_specs(wj),
            *cast_in_specs,
        ],
        out_specs=[pl.BlockSpec((rc, d), done_map), *cast_out_specs],
        out_shape=[jax.ShapeDtypeStruct((t, d), F32), *cast_shapes],
        scratch_shapes=[
            pltpu.VMEM((2 * tm, d), BF16),
            pltpu.VMEM((2 * tm, d), F32),
            *extra_scratch,
        ],
        compiler_params=_params(("arbitrary", "arbitrary")),
        name=name,
    )(x, x, shift, scale, gate, g_pre, g_post, *weights, *[w for w, _ in casts])
    return outs[0], outs[1:]


def _mixer_step(tile, blk, part, w_refs, h_ref, acc_ref, tile_rows, extra, side_finish,
                side_norm, *, tiles_per_seq):
    cw_ref, wb_ref, wc_ref, wu_ref, wo_ref = w_refs
    zbuf_ref, carry_ref = extra
    tm = tile_rows.size
    cg = _dot(h_ref[tile_rows, :], wc_ref[...])
    side_finish()
    side_norm()
    h = h_ref[tile_rows, :]
    z = cg * _dot(h, wu_ref[...])

    seq_start = ((tile % tiles_per_seq) == 0) & (part == 0)
    zbuf_ref[0:SUBLANES, :] = jnp.where(seq_start, 0.0, carry_ref[blk])
    zbuf_ref[SUBLANES:, :] = z
    carry_ref[blk] = z[tm - SUBLANES:, :]
    cw = cw_ref[...]
    zc = (cw[2:3, :] * z
          + cw[1:2, :] * zbuf_ref[SUBLANES - 1:SUBLANES - 1 + tm, :]
          + cw[0:1, :] * zbuf_ref[SUBLANES - 2:SUBLANES - 2 + tm, :])
    bg = _dot(h_ref[tile_rows, :], wb_ref[...])
    acc_ref[tile_rows, :] += _dot((bg * zc).astype(BF16), wo_ref[...])


def _mixer(x, shift, scale, gate, g_pre, g_post, conv_w, w_in, w_out, casts, seq,
           tm=512, tn=512, row_split=1):
    d = x.shape[1]
    nj = d // tn

    def weight_specs(wj):
        return [
            pl.BlockSpec((3, tn), lambda i, j: (0, wj(i, j))),
            pl.BlockSpec((d, tn), lambda i, j: (0, wj(i, j))),
            pl.BlockSpec((d, tn), lambda i, j: (0, nj + wj(i, j))),
            pl.BlockSpec((d, tn), lambda i, j: (0, 2 * nj + wj(i, j))),
            pl.BlockSpec((tn, d), lambda i, j: (wj(i, j), 0)),
        ]

    return _sublayer(
        x, shift, scale, gate, g_pre, g_post, [conv_w, w_in, w_in, w_in, w_out], weight_specs,
        functools.partial(_mixer_step, tiles_per_seq=seq // tm), casts,
        seq=seq, tm=tm, n_blocks=nj, row_split=row_split,
        extra_scratch=[pltpu.VMEM((tm // row_split + SUBLANES, tn), F32),
                       pltpu.VMEM((nj, SUBLANES, tn), F32)],
        name="conv_mixer")


MLP_SUB = 512


def _mlp_step(tile, blk, part, w_refs, h_ref, acc_ref, tile_rows, extra, side_finish, side_norm):
    wup_ref, wdn_ref = w_refs
    for c in range(0, wup_ref.shape[1], MLP_SUB):
        if c == MLP_SUB:
            side_finish()
            side_norm()
        a = jnp.maximum(_dot(h_ref[tile_rows, :], wup_ref[:, c:c + MLP_SUB]), 0.0)
        acc_ref[tile_rows, :] += _dot((a * a).astype(BF16), wdn_ref[c:c + MLP_SUB, :])


def _mlp(x, shift, scale, gate, g_pre, g_post, w_up, w_down, casts, seq, tm=1024, tf=1024):
    d = x.shape[1]
    dff = w_up.shape[1]

    def weight_specs(wj):
        return [
            pl.BlockSpec((d, tf), lambda i, j: (0, wj(i, j))),
            pl.BlockSpec((tf, d), lambda i, j: (wj(i, j), 0)),
        ]

    return _sublayer(
        x, shift, scale, gate, g_pre, g_post, [w_up, w_down], weight_specs, _mlp_step, casts,
        seq=seq, tm=tm, n_blocks=dff // tf, row_split=1, extra_scratch=[], name="relu2_mlp")


ROW_PARTS = 4


def _rope(t, rows, cos_ref, s1_ref, s2_ref):
    n = t.shape[1]
    reps = n // LANES
    cos = jnp.tile(cos_ref[rows, :], (1, reps))
    s1 = jnp.tile(s1_ref[rows, :], (1, reps))
    s2 = jnp.tile(s2_ref[rows, :], (1, reps))
    half = ROT_DIM // 2
    return (t * cos + pltpu.roll(t, n - half, 1) * s1 + pltpu.roll(t, half, 1) * s2)


def _dup_heads(t):
    lo = lax.broadcasted_iota(jnp.int32, (t.shape[0], LANES), 1) < HEAD_DIM
    out = []
    for m in range(t.shape[1] // LANES):
        slab = t[:, m * LANES:(m + 1) * LANES]
        swapped = pltpu.roll(slab, HEAD_DIM, 1)
        out += [jnp.where(lo, slab, swapped), jnp.where(lo, swapped, slab)]
    return jnp.concatenate(out, axis=1)


def _qkv_kernel(x_ref, shq_ref, scq_ref, gq_ref, shk_ref, sck_ref, gk_ref,
                wq_ref, bq_ref, wkv_ref, bkv_ref,
                cos_ref, s1_ref, s2_ref, q_ref, k_ref, v_ref):
    aq = gq_ref[...] * (1.0 + scq_ref[...])
    ak = gk_ref[...] * (1.0 + sck_ref[...])
    part = x_ref.shape[0] // ROW_PARTS
    for r in range(0, x_ref.shape[0], part):
        rows = slice(r, r + part)
        x = x_ref[rows, :]
        xn = x * _rms_scale(x)
        hq = (xn * aq + shq_ref[...]).astype(BF16)
        hk = (xn * ak + shk_ref[...]).astype(BF16)
        q = _dot(hq, wq_ref[...]) + bq_ref[...]
        q_ref[rows, :] = (_rope(q, rows, cos_ref, s1_ref, s2_ref)
                          * (HEAD_DIM ** -0.5)).astype(BF16)
        kv = _dot(hk, wkv_ref[...]) + bkv_ref[...]
        nk = kv.shape[1] // 2
        k_ref[rows, :] = _dup_heads(_rope(kv[:, :nk], rows, cos_ref, s1_ref, s2_ref)).astype(BF16)
        v_ref[rows, :] = _dup_heads(kv[:, nk:]).astype(BF16)


def _qkv(x, shq, scq, gq, shk, sck, gk, wq, bq, wkv, bkv, cos, s1, s2, seq, tm=1024):
    t, d = x.shape
    nq = wq.shape[1]
    nk = wkv.shape[1]
    tps = seq // tm
    mod_spec = pl.BlockSpec((None, 1, d), lambda i: (i // tps, 0, 0))
    vec_spec = pl.BlockSpec((1, d), lambda i: (0, 0))
    tab_spec = pl.BlockSpec((tm, LANES), lambda i: (i % tps, 0))

    def full(a):
        return pl.BlockSpec(a.shape, lambda i: (0, 0))

    return pl.pallas_call(
        _qkv_kernel,
        grid=(t // tm,),
        in_specs=[
            pl.BlockSpec((tm, d), lambda i: (i, 0)),
            mod_spec, mod_spec, vec_spec, mod_spec, mod_spec, vec_spec,
            full(wq), full(bq), full(wkv), full(bkv),
            tab_spec, tab_spec, tab_spec,
        ],
        out_specs=[
            pl.BlockSpec((tm, nq), lambda i: (i, 0)),
            pl.BlockSpec((tm, nk), lambda i: (i, 0)),
            pl.BlockSpec((tm, nk), lambda i: (i, 0)),
        ],
        out_shape=[
            jax.ShapeDtypeStruct((t, nq), BF16),
            jax.ShapeDtypeStruct((t, nk), BF16),
            jax.ShapeDtypeStruct((t, nk), BF16),
        ],
        compiler_params=_params(("parallel",)),
        name="qkv_proj",
    )(x, shq, scq, gq, shk, sck, gk, wq, bq, wkv, bkv, cos, s1, s2)


def _attn_kernel(sinks_ref, q_ref, kp_ref, kc_ref, vp_ref, vc_ref, o_ref, *, q_blocks):
    n = pl.program_id(1)
    blk = ATT_BLOCK
    n_kv = kp_ref.shape[1] // LANES
    pairs_per_kv = KV_GROUP // 2

    qi = lax.broadcasted_iota(jnp.int32, (2 * blk, LANES), 0) & (blk - 1)
    kc = lax.broadcasted_iota(jnp.int32, (2 * blk, LANES), 1)
    from_prev = kc > qi
    lo = lax.broadcasted_iota(jnp.int32, (blk, LANES), 1) < HEAD_DIM
    top = lax.broadcasted_iota(jnp.int32, (2 * blk, 1), 0) < blk
    ones = jnp.ones((blk, LANES), BF16)
    no_prev_bias = jnp.where(n == 0, -jnp.inf, 0.0)

    for g in range(n_kv):
        cols = slice(g * LANES, (g + 1) * LANES)
        for b in range(q_blocks):
            cur = slice(b * blk, (b + 1) * blk)
            if b == 0:
                k_prev, v_prev = kp_ref[:, cols], vp_ref[:, cols]
            else:
                prv = slice((b - 1) * blk, b * blk)
                k_prev, v_prev = kc_ref[prv, cols], vc_ref[prv, cols]
            kband = jnp.concatenate([k_prev, kc_ref[cur, cols]], axis=0)
            vext = jnp.concatenate(
                [jnp.concatenate([v_prev, ones], axis=1),
                 jnp.concatenate([vc_ref[cur, cols], ones], axis=1)], axis=0)
            for pp in range(pairs_per_kv):
                p = g * pairs_per_kv + pp
                pcols = slice(p * LANES, (p + 1) * LANES)
                qp = q_ref[cur, pcols]
                zero = jnp.zeros_like(qp)
                q2 = jnp.concatenate([jnp.where(lo, qp, zero), jnp.where(lo, zero, qp)], axis=0)
                s = lax.dot_general(q2, kband, (((1,), (1,)), ((), ())),
                                    preferred_element_type=F32)
                s_prev = s[:, :LANES]
                if b == 0:
                    s_prev = s_prev + no_prev_bias
                sc = jnp.where(from_prev, s_prev, s[:, LANES:])
                sink = jnp.where(top, sinks_ref[2 * p], sinks_ref[2 * p + 1])
                m = jnp.maximum(jnp.max(sc, axis=-1, keepdims=True), sink)
                e = jnp.exp(sc - m).astype(BF16)
                ez = jnp.zeros_like(e)
                pmat = jnp.concatenate([jnp.where(from_prev, e, ez),
                                        jnp.where(from_prev, ez, e)], axis=1)
                pv = _dot(pmat, vext)
                den = pv[:, LANES:] + jnp.exp(sink - m)
                num = jnp.where(lo, pv[:blk, :LANES], pv[blk:, :LANES])
                o_ref[cur, pcols] = (num / jnp.where(lo, den[:blk], den[blk:])).astype(BF16)


def _attention(sinks, q, kd, vd, seq, q_blocks=4):
    t, nq = q.shape
    nk = kd.shape[1]
    tq = q_blocks * ATT_BLOCK
    steps = seq // tq
    batch = t // seq
    cur = lambda b, n: (b * steps + n, 0)
    prev = lambda b, n: ((b * steps + n) * q_blocks - jnp.where(n == 0, 0, 1), 0)
    return pl.pallas_call(
        functools.partial(_attn_kernel, q_blocks=q_blocks),
        grid=(batch, steps),
        in_specs=[
            pl.BlockSpec(memory_space=pltpu.SMEM),
            pl.BlockSpec((tq, nq), cur),
            pl.BlockSpec((ATT_BLOCK, nk), prev),
            pl.BlockSpec((tq, nk), cur),
            pl.BlockSpec((ATT_BLOCK, nk), prev),
            pl.BlockSpec((tq, nk), cur),
        ],
        out_specs=pl.BlockSpec((tq, nq), cur),
        out_shape=jax.ShapeDtypeStruct((t, nq), BF16),
        compiler_params=_params(("parallel", "parallel")),
        name="swa_attention",
    )(sinks, q, kd, kd, vd, vd)


def _oproj_kernel(x_ref, o_ref, gt_ref, gpost_ref, wo_ref, bo_ref, out_ref):
    gg = gt_ref[...] * gpost_ref[...]
    part = x_ref.shape[0] // ROW_PARTS
    for r in range(0, x_ref.shape[0], part):
        rows = slice(r, r + part)
        y = _dot(o_ref[rows, :], wo_ref[...]) + bo_ref[...]
        out_ref[rows, :] = x_ref[rows, :] + y * _rms_scale(y) * gg


def _oproj(x, o, gate, g_post, w_o, b_o, seq, tm=1024):
    t, d = x.shape
    nq = o.shape[1]
    tps = seq // tm
    return pl.pallas_call(
        _oproj_kernel,
        grid=(t // tm,),
        in_specs=[
            pl.BlockSpec((tm, d), lambda i: (i, 0)),
            pl.BlockSpec((tm, nq), lambda i: (i, 0)),
            pl.BlockSpec((None, 1, d), lambda i: (i // tps, 0, 0)),
            pl.BlockSpec((1, d), lambda i: (0, 0)),
            pl.BlockSpec((nq, d), lambda i: (0, 0)),
            pl.BlockSpec((1, d), lambda i: (0, 0)),
        ],
        out_specs=pl.BlockSpec((tm, d), lambda i: (i, 0)),
        out_shape=jax.ShapeDtypeStruct((t, d), F32),
        compiler_params=_params(("parallel",)),
        name="attn_out_proj",
    )(x, o, gate, g_post, w_o, b_o)


def _rope_tables(seq):
    half = ROT_DIM // 2
    inv = ROPE_THETA ** (-jnp.arange(0, ROT_DIM, 2, dtype=F32) / ROT_DIM)
    ang = jnp.arange(seq, dtype=F32)[:, None] * inv[None, :]
    cos, sin = jnp.cos(ang), jnp.sin(ang)
    ones = jnp.ones((seq, HEAD_DIM - ROT_DIM), F32)
    zeros = jnp.zeros((seq, HEAD_DIM - ROT_DIM), F32)
    zh = jnp.zeros((seq, half), F32)
    c_tab = jnp.concatenate([cos, cos, ones], axis=1)
    s1_tab = jnp.concatenate([-sin, zh, zeros], axis=1)
    s2_tab = jnp.concatenate([zh, sin, zeros], axis=1)
    rep = LANES // HEAD_DIM
    return tuple(jnp.tile(a, (1, rep)) for a in (c_tab, s1_tab, s2_tab))


def kernel(x, c, ada_w, ada_b, norm_pre, norm_post, conv_w_in, conv_w, conv_w_out,
           kv_ada_w, kv_ada_b, kv_norm, w_kv, b_kv, w_q, b_q, sinks, w_o, b_o,
           mlp_up, mlp_down):
    batch, seq, d = x.shape
    depth = ada_w.shape[0]
    n_self = conv_w_in.shape[0]
    t = batch * seq
    assert n_self == 1 and depth == 2, "one conv-mixer layer then one shared-KV attention layer"

    c_pad = jnp.pad(c, ((0, SUBLANES - batch), (0, 0)))
    mods = _ada(c_pad, ada_w.reshape(depth * 2, d, 3 * d), ada_b.reshape(depth * 2, 1, 3 * d))
    mods = mods[:, :batch].reshape(depth, 2, batch, 3, 1, d)
    kv_mods = _ada(c_pad, kv_ada_w[None], kv_ada_b[None, None])
    kv_mods = kv_mods[0, :batch].reshape(batch, 2, 1, d)

    def mod(l, sub):
        return [mods[l, sub, :, which] for which in range(3)]

    xf = x.reshape(t, d)

    xf, (up0, down0) = _mixer(
        xf, *mod(0, 0), norm_pre[0, 0:1], norm_post[0, 0:1], conv_w[0],
        conv_w_in[0].astype(BF16), conv_w_out[0].astype(BF16),
        [(mlp_up, 0), (mlp_down, 0)], seq)
    xf, (wq_bf, wo_bf, up1, down1) = _mlp(
        xf, *mod(0, 1), norm_pre[0, 1:2], norm_post[0, 1:2], up0, down0,
        [(w_q, 0), (w_o, 0), (mlp_up, 1), (mlp_down, 1)], seq)

    cos, s1, s2 = _rope_tables(seq)
    shift, scale, gate = mod(1, 0)
    q, kd, vd = _qkv(xf, shift, scale, norm_pre[1, 0:1],
                     kv_mods[:, 0], kv_mods[:, 1], kv_norm[None],
                     wq_bf, b_q[0][None], w_kv.astype(BF16), b_kv[None], cos, s1, s2, seq)
    o = _attention(sinks[0], q, kd, vd, seq)
    xf = _oproj(xf, o, gate, norm_post[1, 0:1], wo_bf, b_o[0][None], seq)
    xf, _ = _mlp(xf, *mod(1, 1), norm_pre[1, 1:2], norm_post[1, 1:2], up1, down1, [], seq)
    return xf.reshape(batch, seq, d)
```

```python
import functools

import jax
import jax.numpy as jnp
from jax import lax
from jax.experimental import pallas as pl
from jax.experimental.pallas import tpu as pltpu

EPS = 1e-6
HEAD_DIM = 64
KV_GROUP = 8
ROT_DIM = HEAD_DIM // 4
ROPE_THETA = 500000.0
ATT_BLOCK = 128
LANES = 128
SUBLANES = 8
VMEM_LIMIT = 56 * 1024 * 1024
ROW_SUB = 32

BF16 = jnp.bfloat16
F32 = jnp.float32


def _dot(a, b):
    return jnp.dot(a, b, preferred_element_type=F32)


def _rms_scale(x):
    return lax.rsqrt(jnp.mean(x * x, axis=-1, keepdims=True) + EPS)


def _params(sem):
    return pltpu.CompilerParams(dimension_semantics=sem, vmem_limit_bytes=VMEM_LIMIT)


def _ada_kernel(c_ref, w_ref, b_ref, o_ref):
    c = c_ref[...]
    c_act = c / (1.0 + jnp.exp(-c))
    o_ref[...] = _dot(c_act.astype(BF16), w_ref[...].astype(BF16)) + b_ref[...]


def _ada(c_pad, w, b, tn=2048):
    g, d, n = w.shape
    return pl.pallas_call(
        _ada_kernel,
        grid=(g, n // tn),
        in_specs=[
            pl.BlockSpec((SUBLANES, d), lambda a, j: (0, 0)),
            pl.BlockSpec((None, d, tn), lambda a, j: (a, 0, j)),
            pl.BlockSpec((None, 1, tn), lambda a, j: (a, 0, j)),
        ],
        out_specs=pl.BlockSpec((None, SUBLANES, tn), lambda a, j: (a, 0, j)),
        out_shape=jax.ShapeDtypeStruct((g, SUBLANES, n), F32),
        compiler_params=_params(("parallel", "parallel")),
        name="ada_mod",
    )(c_pad, w, b)


def _sublayer_kernel(*refs, n_tiles, row_split, n_w, n_cast, step_fn):
    xn_ref, xr_ref, sh_ref, sc_ref, gt_ref, gpre_ref, gpost_ref = refs[:7]
    k = 7
    w_refs = refs[k:k + n_w]
    k += n_w
    cast_in = refs[k:k + n_cast]
    k += n_cast
    out_ref = refs[k]
    cast_out = refs[k + 1:k + 1 + n_cast]
    k += 1 + n_cast
    h_ref, acc_ref = refs[k:k + 2]
    extra = refs[k + 2:]

    i = pl.program_id(0)
    j = pl.program_id(1)
    rc, d = xn_ref.shape
    tm = h_ref.shape[0] // 2
    row0 = j * rc

    def tile_base(t):
        if isinstance(t, int):
            return (t % 2) * tm
        return pl.multiple_of((t % 2) * tm, tm)

    def rows(base, s, size):
        return pl.ds(pl.multiple_of(base + row0 + s, size), size)

    def normalise(base):
        a = gpre_ref[...] * (1.0 + sc_ref[...])
        sh = sh_ref[...]
        for s in range(0, rc, ROW_SUB):
            x = xn_ref[s:s + ROW_SUB, :]
            h_ref[rows(base, s, ROW_SUB), :] = (x * _rms_scale(x) * a + sh).astype(BF16)

    def finish(base):
        gg = gt_ref[...] * gpost_ref[...]
        for s in range(0, rc, ROW_SUB):
            y = acc_ref[rows(base, s, ROW_SUB), :]
            out_ref[s:s + ROW_SUB, :] = xr_ref[s:s + ROW_SUB, :] + y * _rms_scale(y) * gg

    def clear(base):
        acc_ref[rows(base, 0, rc), :] = jnp.zeros((rc, d), F32)

    @pl.when(i == 0)
    def _():
        normalise(tile_base(0))
        clear(tile_base(0))
        clear(tile_base(1))

    @pl.when((i >= 1) & (i <= n_tiles))
    def _():
        def side_finish():
            finish(tile_base(i))
            clear(tile_base(i))

        def side_norm():
            normalise(tile_base(i))
            for src, dst in zip(cast_in, cast_out):
                dst[...] = src[...].astype(BF16)

        part = tm // row_split
        part_rows = pl.ds(pl.multiple_of(tile_base(i - 1) + (j % row_split) * part, part), part)
        step_fn(i - 1, j // row_split, j % row_split, w_refs, h_ref, acc_ref, part_rows, extra,
                side_finish, side_norm)

    @pl.when(i == n_tiles + 1)
    def _():
        finish(tile_base(n_tiles - 1))


def _sublayer(x, shift, scale, gate, g_pre, g_post, weights, weight_specs, step_fn, casts,
              *, seq, tm, n_blocks, row_split, extra_scratch, name):
    t, d = x.shape
    n = t // tm
    nj = n_blocks * row_split
    rc = tm // nj
    tps = seq // tm
    steps = n * nj

    def norm_tile(i):
        return jnp.minimum(i, n - 1)

    def done_tile(i):
        return jnp.clip(i - 2, 0, n - 1)

    def done_map(i, j):
        return (done_tile(i) * nj + jnp.where(i < 2, 0, j), 0)

    def wj(i, j):
        return jnp.where(i == 0, 0, jnp.where(i == n + 1, n_blocks - 1, j // row_split))

    def cast_step(i, j):
        return jnp.clip((i - 1) * nj + j, 0, steps - 1)

    pre_mod = pl.BlockSpec((None, 1, d), lambda i, j: (norm_tile(i) // tps, 0, 0))
    post_mod = pl.BlockSpec((None, 1, d), lambda i, j: (done_tile(i) // tps, 0, 0))
    vec = pl.BlockSpec((1, d), lambda i, j: (0, 0))
    cast_in_specs, cast_out_specs, cast_shapes = [], [], []
    for w, layer in casts:
        _, r, c = w.shape
        assert r % (steps * 2 * SUBLANES) == 0, "cast row blocks must be whole bf16 tiles"
        cast_in_specs.append(pl.BlockSpec(
            (None, r // steps, c), lambda i, j, layer=layer: (layer, cast_step(i, j), 0)))
        cast_out_specs.append(pl.BlockSpec((r // steps, c), lambda i, j: (cast_step(i, j), 0)))
        cast_shapes.append(jax.ShapeDtypeStruct((r, c), BF16))

    outs = pl.pallas_call(
        functools.partial(_sublayer_kernel, n_tiles=n, row_split=row_split, n_w=len(weights),
                          n_cast=len(casts), step_fn=step_fn),
        grid=(n + 2, nj),
        in_specs=[
            pl.BlockSpec((rc, d), lambda i, j: (norm_tile(i) * nj + j, 0)),
            pl.BlockSpec((rc, d), done_map),
            pre_mod, pre_mod, post_mod, vec, vec,
            *weight_specs(wj),
            *cast_in_specs,
        ],
        out_specs=[pl.BlockSpec((rc, d), done_map), *cast_out_specs],
        out_shape=[jax.ShapeDtypeStruct((t, d), F32), *cast_shapes],
        scratch_shapes=[
            pltpu.VMEM((2 * tm, d), BF16),
            pltpu.VMEM((2 * tm, d), F32),
            *extra_scratch,
        ],
        compiler_params=_params(("arbitrary", "arbitrary")),
        name=name,
    )(x, x, shift, scale, gate, g_pre, g_post, *weights, *[w for w, _ in casts])
    return outs[0], outs[1:]


def _mixer_step(tile, blk, part, w_refs, h_ref, acc_ref, tile_rows, extra, side_finish,
                side_norm, *, tiles_per_seq):
    cw_ref, wb_ref, wc_ref, wu_ref, wo_ref = w_refs
    zbuf_ref, carry_ref = extra
    tm = tile_rows.size
    cg = _dot(h_ref[tile_rows, :], wc_ref[...])
    side_finish()
    side_norm()
    h = h_ref[tile_rows, :]
    z = cg * _dot(h, wu_ref[...])

    seq_start = ((tile % tiles_per_seq) == 0) & (part == 0)
    zbuf_ref[0:SUBLANES, :] = jnp.where(seq_start, 0.0, carry_ref[blk])
    zbuf_ref[SUBLANES:, :] = z
    carry_ref[blk] = z[tm - SUBLANES:, :]
    cw = cw_ref[...]
    zc = (cw[2:3, :] * z
          + cw[1:2, :] * zbuf_ref[SUBLANES - 1:SUBLANES - 1 + tm, :]
          + cw[0:1, :] * zbuf_ref[SUBLANES - 2:SUBLANES - 2 + tm, :])
    bg = _dot(h_ref[tile_rows, :], wb_ref[...])
    acc_ref[tile_rows, :] += _dot((bg * zc).astype(BF16), wo_ref[...])


def _mixer(x, shift, scale, gate, g_pre, g_post, conv_w, w_in, w_out, casts, seq,
           tm=512, tn=512, row_split=1):
    d = x.shape[1]
    nj = d // tn

    def weight_specs(wj):
        return [
            pl.BlockSpec((3, tn), lambda i, j: (0, wj(i, j))),
            pl.BlockSpec((d, tn), lambda i, j: (0, wj(i, j))),
            pl.BlockSpec((d, tn), lambda i, j: (0, nj + wj(i, j))),
            pl.BlockSpec((d, tn), lambda i, j: (0, 2 * nj + wj(i, j))),
            pl.BlockSpec((tn, d), lambda i, j: (wj(i, j), 0)),
        ]

    return _sublayer(
        x, shift, scale, gate, g_pre, g_post, [conv_w, w_in, w_in, w_in, w_out], weight_specs,
        functools.partial(_mixer_step, tiles_per_seq=seq // tm), casts,
        seq=seq, tm=tm, n_blocks=nj, row_split=row_split,
        extra_scratch=[pltpu.VMEM((tm // row_split + SUBLANES, tn), F32),
                       pltpu.VMEM((nj, SUBLANES, tn), F32)],
        name="conv_mixer")


MLP_SUB = 512


def _mlp_step(tile, blk, part, w_refs, h_ref, acc_ref, tile_rows, extra, side_finish, side_norm):
    wup_ref, wdn_ref = w_refs
    for c in range(0, wup_ref.shape[1], MLP_SUB):
        if c == MLP_SUB:
            side_finish()
            side_norm()
        a = jnp.maximum(_dot(h_ref[tile_rows, :], wup_ref[:, c:c + MLP_SUB]), 0.0)
        acc_ref[tile_rows, :] += _dot((a * a).astype(BF16), wdn_ref[c:c + MLP_SUB, :])


def _mlp(x, shift, scale, gate, g_pre, g_post, w_up, w_down, casts, seq, tm=1024, tf=1024):
    d = x.shape[1]
    dff = w_up.shape[1]

    def weight_specs(wj):
        return [
            pl.BlockSpec((d, tf), lambda i, j: (0, wj(i, j))),
            pl.BlockSpec((tf, d), lambda i, j: (wj(i, j), 0)),
        ]

    return _sublayer(
        x, shift, scale, gate, g_pre, g_post, [w_up, w_down], weight_specs, _mlp_step, casts,
        seq=seq, tm=tm, n_blocks=dff // tf, row_split=1, extra_scratch=[], name="relu2_mlp")


ROW_PARTS = 4


def _rope(t, rows, cos_ref, s1_ref, s2_ref):
    n = t.shape[1]
    reps = n // LANES
    cos = jnp.tile(cos_ref[rows, :], (1, reps))
    s1 = jnp.tile(s1_ref[rows, :], (1, reps))
    s2 = jnp.tile(s2_ref[rows, :], (1, reps))
    half = ROT_DIM // 2
    return (t * cos + pltpu.roll(t, n - half, 1) * s1 + pltpu.roll(t, half, 1) * s2)


def _dup_heads(t):
    lo = lax.broadcasted_iota(jnp.int32, (t.shape[0], LANES), 1) < HEAD_DIM
    out = []
    for m in range(t.shape[1] // LANES):
        slab = t[:, m * LANES:(m + 1) * LANES]
        swapped = pltpu.roll(slab, HEAD_DIM, 1)
        out += [jnp.where(lo, slab, swapped), jnp.where(lo, swapped, slab)]
    return jnp.concatenate(out, axis=1)


Q_COLS = 512


def _qkv_kernel(x_ref, shq_ref, scq_ref, gq_ref, shk_ref, sck_ref, gk_ref,
                wq_ref, bq_ref, wkv_ref, bkv_ref,
                cos_ref, s1_ref, s2_ref, q_ref, k_ref, v_ref):
    aq = gq_ref[...] * (1.0 + scq_ref[...])
    ak = gk_ref[...] * (1.0 + sck_ref[...])
    part = x_ref.shape[0] // ROW_PARTS
    for r in range(0, x_ref.shape[0], part):
        rows = slice(r, r + part)
        x = x_ref[rows, :]
        xn = x * _rms_scale(x)
        hq = (xn * aq + shq_ref[...]).astype(BF16)
        hk = (xn * ak + shk_ref[...]).astype(BF16)
        for c in range(0, wq_ref.shape[1], Q_COLS):
            q = _dot(hq, wq_ref[:, c:c + Q_COLS]) + bq_ref[:, c:c + Q_COLS]
            q_ref[rows, c:c + Q_COLS] = (_rope(q, rows, cos_ref, s1_ref, s2_ref)
                                         * (HEAD_DIM ** -0.5)).astype(BF16)
        kv = _dot(hk, wkv_ref[...]) + bkv_ref[...]
        nk = kv.shape[1] // 2
        k_ref[rows, :] = _dup_heads(_rope(kv[:, :nk], rows, cos_ref, s1_ref, s2_ref)).astype(BF16)
        v_ref[rows, :] = _dup_heads(kv[:, nk:]).astype(BF16)


def _qkv(x, shq, scq, gq, shk, sck, gk, wq, bq, wkv, bkv, cos, s1, s2, seq, tm=1024):
    t, d = x.shape
    nq = wq.shape[1]
    nk = wkv.shape[1]
    tps = seq // tm
    mod_spec = pl.BlockSpec((None, 1, d), lambda i: (i // tps, 0, 0))
    vec_spec = pl.BlockSpec((1, d), lambda i: (0, 0))
    tab_spec = pl.BlockSpec((tm, LANES), lambda i: (i % tps, 0))

    def full(a):
        return pl.BlockSpec(a.shape, lambda i: (0, 0))

    return pl.pallas_call(
        _qkv_kernel,
        grid=(t // tm,),
        in_specs=[
            pl.BlockSpec((tm, d), lambda i: (i, 0)),
            mod_spec, mod_spec, vec_spec, mod_spec, mod_spec, vec_spec,
            full(wq), full(bq), full(wkv), full(bkv),
            tab_spec, tab_spec, tab_spec,
        ],
        out_specs=[
            pl.BlockSpec((tm, nq), lambda i: (i, 0)),
            pl.BlockSpec((tm, nk), lambda i: (i, 0)),
            pl.BlockSpec((tm, nk), lambda i: (i, 0)),
        ],
        out_shape=[
            jax.ShapeDtypeStruct((t, nq), BF16),
            jax.ShapeDtypeStruct((t, nk), BF16),
            jax.ShapeDtypeStruct((t, nk), BF16),
        ],
        compiler_params=_params(("parallel",)),
        name="qkv_proj",
    )(x, shq, scq, gq, shk, sck, gk, wq, bq, wkv, bkv, cos, s1, s2)


def _attn_kernel(sinks_ref, q_ref, kp_ref, kc_ref, vp_ref, vc_ref, o_ref, *, q_blocks):
    n = pl.program_id(1)
    blk = ATT_BLOCK
    n_kv = kp_ref.shape[1] // LANES
    pairs_per_kv = KV_GROUP // 2

    qi = lax.broadcasted_iota(jnp.int32, (2 * blk, LANES), 0) & (blk - 1)
    kc = lax.broadcasted_iota(jnp.int32, (2 * blk, LANES), 1)
    from_prev = kc > qi
    lo = lax.broadcasted_iota(jnp.int32, (blk, LANES), 1) < HEAD_DIM
    top = lax.broadcasted_iota(jnp.int32, (2 * blk, 1), 0) < blk
    ones = jnp.ones((blk, LANES), BF16)
    no_prev_bias = jnp.where(n == 0, -jnp.inf, 0.0)

    for g in range(n_kv):
        cols = slice(g * LANES, (g + 1) * LANES)
        for b in range(q_blocks):
            cur = slice(b * blk, (b + 1) * blk)
            if b == 0:
                k_prev, v_prev = kp_ref[:, cols], vp_ref[:, cols]
            else:
                prv = slice((b - 1) * blk, b * blk)
                k_prev, v_prev = kc_ref[prv, cols], vc_ref[prv, cols]
            kband = jnp.concatenate([k_prev, kc_ref[cur, cols]], axis=0)
            vext = jnp.concatenate(
                [jnp.concatenate([v_prev, ones], axis=1),
                 jnp.concatenate([vc_ref[cur, cols], ones], axis=1)], axis=0)
            for pp in range(pairs_per_kv):
                p = g * pairs_per_kv + pp
                pcols = slice(p * LANES, (p + 1) * LANES)
                qp = q_ref[cur, pcols]
                zero = jnp.zeros_like(qp)
                q2 = jnp.concatenate([jnp.where(lo, qp, zero), jnp.where(lo, zero, qp)], axis=0)
                s = lax.dot_general(q2, kband, (((1,), (1,)), ((), ())),
                                    preferred_element_type=F32)
                s_prev = s[:, :LANES]
                if b == 0:
                    s_prev = s_prev + no_prev_bias
                sc = jnp.where(from_prev, s_prev, s[:, LANES:])
                sink = jnp.where(top, sinks_ref[2 * p], sinks_ref[2 * p + 1])
                m = jnp.maximum(jnp.max(sc, axis=-1, keepdims=True), sink)
                e = jnp.exp(sc - m).astype(BF16)
                ez = jnp.zeros_like(e)
                pmat = jnp.concatenate([jnp.where(from_prev, e, ez),
                                        jnp.where(from_prev, ez, e)], axis=1)
                pv = _dot(pmat, vext)
                den = pv[:, LANES:] + jnp.exp(sink - m)
                num = jnp.where(lo, pv[:blk, :LANES], pv[blk:, :LANES])
                o_ref[cur, pcols] = (num / jnp.where(lo, den[:blk], den[blk:])).astype(BF16)


def _attention(sinks, q, kd, vd, seq, q_blocks=4):
    t, nq = q.shape
    nk = kd.shape[1]
    tq = q_blocks * ATT_BLOCK
    steps = seq // tq
    batch = t // seq
    cur = lambda b, n: (b * steps + n, 0)
    prev = lambda b, n: ((b * steps + n) * q_blocks - jnp.where(n == 0, 0, 1), 0)
    return pl.pallas_call(
        functools.partial(_attn_kernel, q_blocks=q_blocks),
        grid=(batch, steps),
        in_specs=[
            pl.BlockSpec(memory_space=pltpu.SMEM),
            pl.BlockSpec((tq, nq), cur),
            pl.BlockSpec((ATT_BLOCK, nk), prev),
            pl.BlockSpec((tq, nk), cur),
            pl.BlockSpec((ATT_BLOCK, nk), prev),
            pl.BlockSpec((tq, nk), cur),
        ],
        out_specs=pl.BlockSpec((tq, nq), cur),
        out_shape=jax.ShapeDtypeStruct((t, nq), BF16),
        compiler_params=_params(("parallel", "parallel")),
        name="swa_attention",
    )(sinks, q, kd, kd, vd, vd)


def _oproj_kernel(x_ref, o_ref, gt_ref, gpost_ref, wo_ref, bo_ref, out_ref):
    gg = gt_ref[...] * gpost_ref[...]
    part = x_ref.shape[0] // ROW_PARTS
    for r in range(0, x_ref.shape[0], part):
        rows = slice(r, r + part)
        y = _dot(o_ref[rows, :], wo_ref[...]) + bo_ref[...]
        out_ref[rows, :] = x_ref[rows, :] + y * _rms_scale(y) * gg


def _oproj(x, o, gate, g_post, w_o, b_o, seq, tm=1024):
    t, d = x.shape
    nq = o.shape[1]
    tps = seq // tm
    return pl.pallas_call(
        _oproj_kernel,
        grid=(t // tm,),
        in_specs=[
            pl.BlockSpec((tm, d), lambda i: (i, 0)),
            pl.BlockSpec((tm, nq), lambda i: (i, 0)),
            pl.BlockSpec((None, 1, d), lambda i: (i // tps, 0, 0)),
            pl.BlockSpec((1, d), lambda i: (0, 0)),
            pl.BlockSpec((nq, d), lambda i: (0, 0)),
            pl.BlockSpec((1, d), lambda i: (0, 0)),
        ],
        out_specs=pl.BlockSpec((tm, d), lambda i: (i, 0)),
        out_shape=jax.ShapeDtypeStruct((t, d), F32),
        compiler_params=_params(("parallel",)),
        name="attn_out_proj",
    )(x, o, gate, g_post, w_o, b_o)


def _rope_tables(seq):
    half = ROT_DIM // 2
    inv = ROPE_THETA ** (-jnp.arange(0, ROT_DIM, 2, dtype=F32) / ROT_DIM)
    ang = jnp.arange(seq, dtype=F32)[:, None] * inv[None, :]
    cos, sin = jnp.cos(ang), jnp.sin(ang)
    ones = jnp.ones((seq, HEAD_DIM - ROT_DIM), F32)
    zeros = jnp.zeros((seq, HEAD_DIM - ROT_DIM), F32)
    zh = jnp.zeros((seq, half), F32)
    c_tab = jnp.concatenate([cos, cos, ones], axis=1)
    s1_tab = jnp.concatenate([-sin, zh, zeros], axis=1)
    s2_tab = jnp.concatenate([zh, sin, zeros], axis=1)
    rep = LANES // HEAD_DIM
    return tuple(jnp.tile(a, (1, rep)) for a in (c_tab, s1_tab, s2_tab))


def kernel(x, c, ada_w, ada_b, norm_pre, norm_post, conv_w_in, conv_w, conv_w_out,
           kv_ada_w, kv_ada_b, kv_norm, w_kv, b_kv, w_q, b_q, sinks, w_o, b_o,
           mlp_up, mlp_down):
    batch, seq, d = x.shape
    depth = ada_w.shape[0]
    n_self = conv_w_in.shape[0]
    t = batch * seq
    assert n_self == 1 and depth == 2, "one conv-mixer layer then one shared-KV attention layer"

    c_pad = jnp.pad(c, ((0, SUBLANES - batch), (0, 0)))
    mods = _ada(c_pad, ada_w.reshape(depth * 2, d, 3 * d), ada_b.reshape(depth * 2, 1, 3 * d))
    mods = mods[:, :batch].reshape(depth, 2, batch, 3, 1, d)
    kv_mods = _ada(c_pad, kv_ada_w[None], kv_ada_b[None, None])
    kv_mods = kv_mods[0, :batch].reshape(batch, 2, 1, d)

    def mod(l, sub):
        return [mods[l, sub, :, which] for which in range(3)]

    xf = x.reshape(t, d)

    xf, (up0, down0) = _mixer(
        xf, *mod(0, 0), norm_pre[0, 0:1], norm_post[0, 0:1], conv_w[0],
        conv_w_in[0].astype(BF16), conv_w_out[0].astype(BF16),
        [(mlp_up, 0), (mlp_down, 0)], seq)
    xf, (wq_bf, wo_bf, up1, down1) = _mlp(
        xf, *mod(0, 1), norm_pre[0, 1:2], norm_post[0, 1:2], up0, down0,
        [(w_q, 0), (w_o, 0), (mlp_up, 1), (mlp_down, 1)], seq)

    cos, s1, s2 = _rope_tables(seq)
    shift, scale, gate = mod(1, 0)
    q, kd, vd = _qkv(xf, shift, scale, norm_pre[1, 0:1],
                     kv_mods[:, 0], kv_mods[:, 1], kv_norm[None],
                     wq_bf, b_q[0][None], w_kv.astype(BF16), b_kv[None], cos, s1, s2, seq)
    o = _attention(sinks[0], q, kd, vd, seq)
    xf = _oproj(xf, o, gate, norm_post[1, 0:1], wo_bf, b_o[0][None], seq)
    xf, _ = _mlp(xf, *mod(1, 1), norm_pre[1, 1:2], norm_post[1, 1:2], up1, down1, [], seq)
    return xf.reshape(batch, seq, d)
```

```python
import functools

import jax
import jax.numpy as jnp
from jax import lax
from jax.experimental import pallas as pl
from jax.experimental.pallas import tpu as pltpu

EPS = 1e-6
HEAD_DIM = 64
KV_GROUP = 8
ROT_DIM = HEAD_DIM // 4
ROPE_THETA = 500000.0
ATT_BLOCK = 128
LANES = 128
SUBLANES = 8
VMEM_LIMIT = 56 * 1024 * 1024
ROW_SUB = 32

BF16 = jnp.bfloat16
F32 = jnp.float32


def _dot(a, b):
    return jnp.dot(a, b, preferred_element_type=F32)


def _rms_scale(x):
    return lax.rsqrt(jnp.mean(x * x, axis=-1, keepdims=True) + EPS)


def _params(sem):
    return pltpu.CompilerParams(dimension_semantics=sem, vmem_limit_bytes=VMEM_LIMIT)


def _ada_kernel(c_ref, w_ref, b_ref, o_ref):
    c = c_ref[...]
    c_act = c / (1.0 + jnp.exp(-c))
    o_ref[...] = _dot(c_act.astype(BF16), w_ref[...].astype(BF16)) + b_ref[...]


def _ada(c_pad, w, b, tn=2048):
    g, d, n = w.shape
    return pl.pallas_call(
        _ada_kernel,
        grid=(g, n // tn),
        in_specs=[
            pl.BlockSpec((SUBLANES, d), lambda a, j: (0, 0)),
            pl.BlockSpec((None, d, tn), lambda a, j: (a, 0, j)),
            pl.BlockSpec((None, 1, tn), lambda a, j: (a, 0, j)),
        ],
        out_specs=pl.BlockSpec((None, SUBLANES, tn), lambda a, j: (a, 0, j)),
        out_shape=jax.ShapeDtypeStruct((g, SUBLANES, n), F32),
        compiler_params=_params(("parallel", "parallel")),
        name="ada_mod",
    )(c_pad, w, b)


def _sublayer_kernel(*refs, n_tiles, n_w, n_cast, step_fn):
    xn_ref, xr_ref, sh_ref, sc_ref, gt_ref, gpre_ref, gpost_ref = refs[:7]
    k = 7
    w_refs = refs[k:k + n_w]
    k += n_w
    cast_in = refs[k:k + n_cast]
    k += n_cast
    out_ref = refs[k]
    cast_out = refs[k + 1:k + 1 + n_cast]
    k += 1 + n_cast
    h_ref, acc_ref = refs[k:k + 2]
    extra = refs[k + 2:]

    i = pl.program_id(0)
    j = pl.program_id(1)
    rc, d = xn_ref.shape
    tm = h_ref.shape[0] // 2
    row0 = j * rc

    def tile_base(t):
        if isinstance(t, int):
            return (t % 2) * tm
        return pl.multiple_of((t % 2) * tm, tm)

    def rows(base, s, size):
        return pl.ds(pl.multiple_of(base + row0 + s, size), size)

    def normalise(base):
        a = gpre_ref[...] * (1.0 + sc_ref[...])
        sh = sh_ref[...]
        for s in range(0, rc, ROW_SUB):
            x = xn_ref[s:s + ROW_SUB, :]
            h_ref[rows(base, s, ROW_SUB), :] = (x * _rms_scale(x) * a + sh).astype(BF16)

    def finish(base):
        gg = gt_ref[...] * gpost_ref[...]
        for s in range(0, rc, ROW_SUB):
            y = acc_ref[rows(base, s, ROW_SUB), :]
            out_ref[s:s + ROW_SUB, :] = xr_ref[s:s + ROW_SUB, :] + y * _rms_scale(y) * gg

    def clear(base):
        acc_ref[rows(base, 0, rc), :] = jnp.zeros((rc, d), F32)

    @pl.when(i == 0)
    def _():
        normalise(tile_base(0))
        clear(tile_base(0))
        clear(tile_base(1))

    @pl.when((i >= 1) & (i <= n_tiles))
    def _():
        def side_work():
            finish(tile_base(i))
            clear(tile_base(i))
            normalise(tile_base(i))
            for src, dst in zip(cast_in, cast_out):
                dst[...] = src[...].astype(BF16)

        step_fn(i - 1, j, w_refs, h_ref, acc_ref, pl.ds(tile_base(i - 1), tm), extra, side_work)

    @pl.when(i == n_tiles + 1)
    def _():
        finish(tile_base(n_tiles - 1))


def _sublayer(x, shift, scale, gate, g_pre, g_post, weights, weight_specs, step_fn, casts,
              *, seq, tm, nj, extra_scratch, name):
    t, d = x.shape
    n = t // tm
    rc = tm // nj
    tps = seq // tm
    steps = n * nj

    def norm_tile(i):
        return jnp.minimum(i, n - 1)

    def done_tile(i):
        return jnp.clip(i - 2, 0, n - 1)

    def done_map(i, j):
        return (done_tile(i) * nj + jnp.where(i < 2, 0, j), 0)

    def wj(i, j):
        return jnp.where(i == 0, 0, jnp.where(i == n + 1, nj - 1, j))

    def cast_step(i, j):
        return jnp.clip((i - 1) * nj + j, 0, steps - 1)

    pre_mod = pl.BlockSpec((None, 1, d), lambda i, j: (norm_tile(i) // tps, 0, 0))
    post_mod = pl.BlockSpec((None, 1, d), lambda i, j: (done_tile(i) // tps, 0, 0))
    vec = pl.BlockSpec((1, d), lambda i, j: (0, 0))
    cast_in_specs, cast_out_specs, cast_shapes = [], [], []
    for w, layer in casts:
        _, r, c = w.shape
        assert r % (steps * 2 * SUBLANES) == 0, "cast row blocks must be whole bf16 tiles"
        cast_in_specs.append(pl.BlockSpec(
            (None, r // steps, c), lambda i, j, layer=layer: (layer, cast_step(i, j), 0)))
        cast_out_specs.append(pl.BlockSpec((r // steps, c), lambda i, j: (cast_step(i, j), 0)))
        cast_shapes.append(jax.ShapeDtypeStruct((r, c), BF16))

    outs = pl.pallas_call(
        functools.partial(_sublayer_kernel, n_tiles=n, n_w=len(weights), n_cast=len(casts),
                          step_fn=step_fn),
        grid=(n + 2, nj),
        in_specs=[
            pl.BlockSpec((rc, d), lambda i, j: (norm_tile(i) * nj + j, 0)),
            pl.BlockSpec((rc, d), done_map),
            pre_mod, pre_mod, post_mod, vec, vec,
            *weight_specs(wj),
            *cast_in_specs,
        ],
        out_specs=[pl.BlockSpec((rc, d), done_map), *cast_out_specs],
        out_shape=[jax.ShapeDtypeStruct((t, d), F32), *cast_shapes],
        scratch_shapes=[
            pltpu.VMEM((2 * tm, d), BF16),
            pltpu.VMEM((2 * tm, d), F32),
            *extra_scratch,
        ],
        compiler_params=_params(("arbitrary", "arbitrary")),
        name=name,
    )(x, x, shift, scale, gate, g_pre, g_post, *weights, *[w for w, _ in casts])
    return outs[0], outs[1:]


def _mixer_step(tile, j, w_refs, h_ref, acc_ref, tile_rows, extra, side_work, *, tiles_per_seq):
    cw_ref, wb_ref, wc_ref, wu_ref, wo_ref = w_refs
    zbuf_ref, carry_ref = extra
    tm = tile_rows.size
    cg = _dot(h_ref[tile_rows, :], wc_ref[...])
    side_work()
    h = h_ref[tile_rows, :]
    z = cg * _dot(h, wu_ref[...])

    seq_start = (tile % tiles_per_seq) == 0
    zbuf_ref[0:SUBLANES, :] = jnp.where(seq_start, 0.0, carry_ref[j])
    zbuf_ref[SUBLANES:, :] = z
    carry_ref[j] = z[tm - SUBLANES:, :]
    cw = cw_ref[...]
    zc = (cw[2:3, :] * z
          + cw[1:2, :] * zbuf_ref[SUBLANES - 1:SUBLANES - 1 + tm, :]
          + cw[0:1, :] * zbuf_ref[SUBLANES - 2:SUBLANES - 2 + tm, :])
    bg = _dot(h_ref[tile_rows, :], wb_ref[...])
    acc_ref[tile_rows, :] += _dot((bg * zc).astype(BF16), wo_ref[...])


def _mixer(x, shift, scale, gate, g_pre, g_post, conv_w, w_in, w_out, casts, seq,
           tm=512, tn=512):
    d = x.shape[1]
    nj = d // tn

    def weight_specs(wj):
        return [
            pl.BlockSpec((3, tn), lambda i, j: (0, wj(i, j))),
            pl.BlockSpec((d, tn), lambda i, j: (0, wj(i, j))),
            pl.BlockSpec((d, tn), lambda i, j: (0, nj + wj(i, j))),
            pl.BlockSpec((d, tn), lambda i, j: (0, 2 * nj + wj(i, j))),
            pl.BlockSpec((tn, d), lambda i, j: (wj(i, j), 0)),
        ]

    return _sublayer(
        x, shift, scale, gate, g_pre, g_post, [conv_w, w_in, w_in, w_in, w_out], weight_specs,
        functools.partial(_mixer_step, tiles_per_seq=seq // tm), casts,
        seq=seq, tm=tm, nj=nj,
        extra_scratch=[pltpu.VMEM((tm + SUBLANES, tn), F32),
                       pltpu.VMEM((nj, SUBLANES, tn), F32)],
        name="conv_mixer")


MLP_SUB = 512


def _mlp_step(tile, j, w_refs, h_ref, acc_ref, tile_rows, extra, side_work):
    wup_ref, wdn_ref = w_refs
    for c in range(0, wup_ref.shape[1], MLP_SUB):
        if c == MLP_SUB:
            side_work()
        a = jnp.maximum(_dot(h_ref[tile_rows, :], wup_ref[:, c:c + MLP_SUB]), 0.0)
        acc_ref[tile_rows, :] += _dot((a * a).astype(BF16), wdn_ref[c:c + MLP_SUB, :])


def _mlp(x, shift, scale, gate, g_pre, g_post, w_up, w_down, casts, seq, tm=1024, tf=1024):
    d = x.shape[1]
    dff = w_up.shape[1]

    def weight_specs(wj):
        return [
            pl.BlockSpec((d, tf), lambda i, j: (0, wj(i, j))),
            pl.BlockSpec((tf, d), lambda i, j: (wj(i, j), 0)),
        ]

    return _sublayer(
        x, shift, scale, gate, g_pre, g_post, [w_up, w_down], weight_specs, _mlp_step, casts,
        seq=seq, tm=tm, nj=dff // tf, extra_scratch=[], name="relu2_mlp")


ROW_PARTS = 4


def _rope(t, rows, cos_ref, s1_ref, s2_ref):
    n = t.shape[1]
    reps = n // LANES
    cos = jnp.tile(cos_ref[rows, :], (1, reps))
    s1 = jnp.tile(s1_ref[rows, :], (1, reps))
    s2 = jnp.tile(s2_ref[rows, :], (1, reps))
    half = ROT_DIM // 2
    return (t * cos + pltpu.roll(t, n - half, 1) * s1 + pltpu.roll(t, half, 1) * s2)


def _dup_heads(t):
    lo = lax.broadcasted_iota(jnp.int32, (t.shape[0], LANES), 1) < HEAD_DIM
    out = []
    for m in range(t.shape[1] // LANES):
        slab = t[:, m * LANES:(m + 1) * LANES]
        swapped = pltpu.roll(slab, HEAD_DIM, 1)
        out += [jnp.where(lo, slab, swapped), jnp.where(lo, swapped, slab)]
    return jnp.concatenate(out, axis=1)


Q_COLS = 512


def _qkv_kernel(x_ref, shq_ref, scq_ref, gq_ref, shk_ref, sck_ref, gk_ref,
                wq_ref, bq_ref, wkv_ref, bkv_ref,
                cos_ref, s1_ref, s2_ref, q_ref, k_ref, v_ref):
    aq = gq_ref[...] * (1.0 + scq_ref[...])
    ak = gk_ref[...] * (1.0 + sck_ref[...])
    part = x_ref.shape[0] // ROW_PARTS
    for r in range(0, x_ref.shape[0], part):
        rows = slice(r, r + part)
        x = x_ref[rows, :]
        xn = x * _rms_scale(x)
        hq = (xn * aq + shq_ref[...]).astype(BF16)
        hk = (xn * ak + shk_ref[...]).astype(BF16)
        for c in range(0, wq_ref.shape[1], Q_COLS):
            q = _dot(hq, wq_ref[:, c:c + Q_COLS]) + bq_ref[:, c:c + Q_COLS]
            q_ref[rows, c:c + Q_COLS] = (_rope(q, rows, cos_ref, s1_ref, s2_ref)
                                         * (HEAD_DIM ** -0.5)).astype(BF16)
        kv = _dot(hk, wkv_ref[...]) + bkv_ref[...]
        nk = kv.shape[1] // 2
        k_ref[rows, :] = _dup_heads(_rope(kv[:, :nk], rows, cos_ref, s1_ref, s2_ref)).astype(BF16)
        v_ref[rows, :] = _dup_heads(kv[:, nk:]).astype(BF16)


def _qkv(x, shq, scq, gq, shk, sck, gk, wq, bq, wkv, bkv, cos, s1, s2, seq, tm=1024):
    t, d = x.shape
    nq = wq.shape[1]
    nk = wkv.shape[1]
    tps = seq // tm
    mod_spec = pl.BlockSpec((None, 1, d), lambda i: (i // tps, 0, 0))
    vec_spec = pl.BlockSpec((1, d), lambda i: (0, 0))
    tab_spec = pl.BlockSpec((tm, LANES), lambda i: (i % tps, 0))

    def full(a):
        return pl.BlockSpec(a.shape, lambda i: (0, 0))

    return pl.pallas_call(
        _qkv_kernel,
        grid=(t // tm,),
        in_specs=[
            pl.BlockSpec((tm, d), lambda i: (i, 0)),
            mod_spec, mod_spec, vec_spec, mod_spec, mod_spec, vec_spec,
            full(wq), full(bq), full(wkv), full(bkv),
            tab_spec, tab_spec, tab_spec,
        ],
        out_specs=[
            pl.BlockSpec((tm, nq), lambda i: (i, 0)),
            pl.BlockSpec((tm, nk), lambda i: (i, 0)),
            pl.BlockSpec((tm, nk), lambda i: (i, 0)),
        ],
        out_shape=[
            jax.ShapeDtypeStruct((t, nq), BF16),
            jax.ShapeDtypeStruct((t, nk), BF16),
            jax.ShapeDtypeStruct((t, nk), BF16),
        ],
        compiler_params=_params(("parallel",)),
        name="qkv_proj",
    )(x, shq, scq, gq, shk, sck, gk, wq, bq, wkv, bkv, cos, s1, s2)


def _attn_kernel(sinks_ref, q_ref, kp_ref, kc_ref, vp_ref, vc_ref, o_ref, *, q_blocks):
    n = pl.program_id(1)
    blk = ATT_BLOCK
    n_kv = kp_ref.shape[1] // LANES
    pairs_per_kv = KV_GROUP // 2

    qi = lax.broadcasted_iota(jnp.int32, (2 * blk, LANES), 0) & (blk - 1)
    kc = lax.broadcasted_iota(jnp.int32, (2 * blk, LANES), 1)
    from_prev = kc > qi
    lo = lax.broadcasted_iota(jnp.int32, (blk, LANES), 1) < HEAD_DIM
    top = lax.broadcasted_iota(jnp.int32, (2 * blk, 1), 0) < blk
    ones = jnp.ones((blk, LANES), BF16)
    no_prev_bias = jnp.where(n == 0, -jnp.inf, 0.0)

    for g in range(n_kv):
        cols = slice(g * LANES, (g + 1) * LANES)
        for b in range(q_blocks):
            cur = slice(b * blk, (b + 1) * blk)
            if b == 0:
                k_prev, v_prev = kp_ref[:, cols], vp_ref[:, cols]
            else:
                prv = slice((b - 1) * blk, b * blk)
                k_prev, v_prev = kc_ref[prv, cols], vc_ref[prv, cols]
            kband = jnp.concatenate([k_prev, kc_ref[cur, cols]], axis=0)
            vext = jnp.concatenate(
                [jnp.concatenate([v_prev, ones], axis=1),
                 jnp.concatenate([vc_ref[cur, cols], ones], axis=1)], axis=0)
            for pp in range(pairs_per_kv):
                p = g * pairs_per_kv + pp
                pcols = slice(p * LANES, (p + 1) * LANES)
                qp = q_ref[cur, pcols]
                zero = jnp.zeros_like(qp)
                q2 = jnp.concatenate([jnp.where(lo, qp, zero), jnp.where(lo, zero, qp)], axis=0)
                s = lax.dot_general(q2, kband, (((1,), (1,)), ((), ())),
                                    preferred_element_type=F32)
                s_prev = s[:, :LANES]
                if b == 0:
                    s_prev = s_prev + no_prev_bias
                sc = jnp.where(from_prev, s_prev, s[:, LANES:])
                sink = jnp.where(top, sinks_ref[2 * p], sinks_ref[2 * p + 1])
                m = jnp.maximum(jnp.max(sc, axis=-1, keepdims=True), sink)
                e = jnp.exp(sc - m).astype(BF16)
                ez = jnp.zeros_like(e)
                pmat = jnp.concatenate([jnp.where(from_prev, e, ez),
                                        jnp.where(from_prev, ez, e)], axis=1)
                pv = _dot(pmat, vext)
                den = pv[:, LANES:] + jnp.exp(sink - m)
                num = jnp.where(lo, pv[:blk, :LANES], pv[blk:, :LANES])
                o_ref[cur, pcols] = (num / jnp.where(lo, den[:blk], den[blk:])).astype(BF16)


def _attention(sinks, q, kd, vd, seq, q_blocks=4):
    t, nq = q.shape
    nk = kd.shape[1]
    tq = q_blocks * ATT_BLOCK
    steps = seq // tq
    batch = t // seq
    cur = lambda b, n: (b * steps + n, 0)
    prev = lambda b, n: ((b * steps + n) * q_blocks - jnp.where(n == 0, 0, 1), 0)
    return pl.pallas_call(
        functools.partial(_attn_kernel, q_blocks=q_blocks),
        grid=(batch, steps),
        in_specs=[
            pl.BlockSpec(memory_space=pltpu.SMEM),
            pl.BlockSpec((tq, nq), cur),
            pl.BlockSpec((ATT_BLOCK, nk), prev),
            pl.BlockSpec((tq, nk), cur),
            pl.BlockSpec((ATT_BLOCK, nk), prev),
            pl.BlockSpec((tq, nk), cur),
        ],
        out_specs=pl.BlockSpec((tq, nq), cur),
        out_shape=jax.ShapeDtypeStruct((t, nq), BF16),
        compiler_params=_params(("parallel", "parallel")),
        name="swa_attention",
    )(sinks, q, kd, kd, vd, vd)


def _oproj_kernel(x_ref, o_ref, gt_ref, gpost_ref, wo_ref, bo_ref, out_ref):
    gg = gt_ref[...] * gpost_ref[...]
    part = x_ref.shape[0] // ROW_PARTS
    for r in range(0, x_ref.shape[0], part):
        rows = slice(r, r + part)
        y = _dot(o_ref[rows, :], wo_ref[...]) + bo_ref[...]
        out_ref[rows, :] = x_ref[rows, :] + y * _rms_scale(y) * gg


def _oproj(x, o, gate, g_post, w_o, b_o, seq, tm=1024):
    t, d = x.shape
    nq = o.shape[1]
    tps = seq // tm
    return pl.pallas_call(
        _oproj_kernel,
        grid=(t // tm,),
        in_specs=[
            pl.BlockSpec((tm, d), lambda i: (i, 0)),
            pl.BlockSpec((tm, nq), lambda i: (i, 0)),
            pl.BlockSpec((None, 1, d), lambda i: (i // tps, 0, 0)),
            pl.BlockSpec((1, d), lambda i: (0, 0)),
            pl.BlockSpec((nq, d), lambda i: (0, 0)),
            pl.BlockSpec((1, d), lambda i: (0, 0)),
        ],
        out_specs=pl.BlockSpec((tm, d), lambda i: (i, 0)),
        out_shape=jax.ShapeDtypeStruct((t, d), F32),
        compiler_params=_params(("parallel",)),
        name="attn_out_proj",
    )(x, o, gate, g_post, w_o, b_o)


def _rope_tables(seq):
    half = ROT_DIM // 2
    inv = ROPE_THETA ** (-jnp.arange(0, ROT_DIM, 2, dtype=F32) / ROT_DIM)
    ang = jnp.arange(seq, dtype=F32)[:, None] * inv[None, :]
    cos, sin = jnp.cos(ang), jnp.sin(ang)
    ones = jnp.ones((seq, HEAD_DIM - ROT_DIM), F32)
    zeros = jnp.zeros((seq, HEAD_DIM - ROT_DIM), F32)
    zh = jnp.zeros((seq, half), F32)
    c_tab = jnp.concatenate([cos, cos, ones], axis=1)
    s1_tab = jnp.concatenate([-sin, zh, zeros], axis=1)
    s2_tab = jnp.concatenate([zh, sin, zeros], axis=1)
    rep = LANES // HEAD_DIM
    return tuple(jnp.tile(a, (1, rep)) for a in (c_tab, s1_tab, s2_tab))


def kernel(x, c, ada_w, ada_b, norm_pre, norm_post, conv_w_in, conv_w, conv_w_out,
           kv_ada_w, kv_ada_b, kv_norm, w_kv, b_kv, w_q, b_q, sinks, w_o, b_o,
           mlp_up, mlp_down):
    batch, seq, d = x.shape
    depth = ada_w.shape[0]
    n_self = conv_w_in.shape[0]
    t = batch * seq
    assert n_self == 1 and depth == 2, "one conv-mixer layer then one shared-KV attention layer"

    c_pad = jnp.pad(c, ((0, SUBLANES - batch), (0, 0)))
    mods = _ada(c_pad, ada_w.reshape(depth * 2, d, 3 * d), ada_b.reshape(depth * 2, 1, 3 * d))
    mods = mods[:, :batch].reshape(depth, 2, batch, 3, 1, d)
    kv_mods = _ada(c_pad, kv_ada_w[None], kv_ada_b[None, None])
    kv_mods = kv_mods[0, :batch].reshape(batch, 2, 1, d)

    def mod(l, sub):
        return [mods[l, sub, :, which] for which in range(3)]

    xf = x.reshape(t, d)

    xf, (up0, down0) = _mixer(
        xf, *mod(0, 0), norm_pre[0, 0:1], norm_post[0, 0:1], conv_w[0],
        conv_w_in[0].astype(BF16), conv_w_out[0].astype(BF16),
        [(mlp_up, 0), (mlp_down, 0)], seq)
    xf, (wq_bf, wo_bf, up1, down1) = _mlp(
        xf, *mod(0, 1), norm_pre[0, 1:2], norm_post[0, 1:2], up0, down0,
        [(w_q, 0), (w_o, 0), (mlp_up, 1), (mlp_down, 1)], seq)

    cos, s1, s2 = _rope_tables(seq)
    shift, scale, gate = mod(1, 0)
    q, kd, vd = _qkv(xf, shift, scale, norm_pre[1, 0:1],
                     kv_mods[:, 0], kv_mods[:, 1], kv_norm[None],
                     wq_bf, b_q[0][None], w_kv.astype(BF16), b_kv[None], cos, s1, s2, seq)
    o = _attention(sinks[0], q, kd, vd, seq)
    xf = _oproj(xf, o, gate, norm_post[1, 0:1], wo_bf, b_o[0][None], seq)
    xf, _ = _mlp(xf, *mod(1, 1), norm_pre[1, 1:2], norm_post[1, 1:2], up1, down1, [], seq)
    return xf.reshape(batch, seq, d)
```

```python
import functools

import jax
import jax.numpy as jnp
from jax import lax
from jax.experimental import pallas as pl
from jax.experimental.pallas import tpu as pltpu

EPS = 1e-6
HEAD_DIM = 64
KV_GROUP = 8
ROT_DIM = HEAD_DIM // 4
ROPE_THETA = 500000.0
ATT_BLOCK = 128
LANES = 128
SUBLANES = 8
VMEM_LIMIT = 56 * 1024 * 1024
ROW_SUB = 32

BF16 = jnp.bfloat16
F32 = jnp.float32


def _dot(a, b):
    return jnp.dot(a, b, preferred_element_type=F32)


def _rms_scale(x):
    return lax.rsqrt(jnp.mean(x * x, axis=-1, keepdims=True) + EPS)


def _params(sem):
    return pltpu.CompilerParams(dimension_semantics=sem, vmem_limit_bytes=VMEM_LIMIT)


def _ada_kernel(c_ref, w_ref, b_ref, o_ref):
    c = c_ref[...]
    c_act = c / (1.0 + jnp.exp(-c))
    o_ref[...] = _dot(c_act.astype(BF16), w_ref[...].astype(BF16)) + b_ref[...]


def _ada(c_pad, w, b, tn=2048):
    g, d, n = w.shape
    return pl.pallas_call(
        _ada_kernel,
        grid=(g, n // tn),
        in_specs=[
            pl.BlockSpec((SUBLANES, d), lambda a, j: (0, 0)),
            pl.BlockSpec((None, d, tn), lambda a, j: (a, 0, j)),
            pl.BlockSpec((None, 1, tn), lambda a, j: (a, 0, j)),
        ],
        out_specs=pl.BlockSpec((None, SUBLANES, tn), lambda a, j: (a, 0, j)),
        out_shape=jax.ShapeDtypeStruct((g, SUBLANES, n), F32),
        compiler_params=_params(("parallel", "parallel")),
        name="ada_mod",
    )(c_pad, w, b)


def _sublayer_kernel(*refs, n_tiles, n_w, n_cast, step_fn):
    xn_ref, xr_ref, sh_ref, sc_ref, gt_ref, gpre_ref, gpost_ref = refs[:7]
    k = 7
    w_refs = refs[k:k + n_w]
    k += n_w
    cast_in = refs[k:k + n_cast]
    k += n_cast
    out_ref = refs[k]
    cast_out = refs[k + 1:k + 1 + n_cast]
    k += 1 + n_cast
    h_ref, acc_ref = refs[k:k + 2]
    extra = refs[k + 2:]

    i = pl.program_id(0)
    j = pl.program_id(1)
    rc, d = xn_ref.shape
    tm = h_ref.shape[0] // 2
    row0 = j * rc

    def tile_base(t):
        if isinstance(t, int):
            return (t % 2) * tm
        return pl.multiple_of((t % 2) * tm, tm)

    def rows(base, s, size):
        return pl.ds(pl.multiple_of(base + row0 + s, size), size)

    def normalise(base):
        a = gpre_ref[...] * (1.0 + sc_ref[...])
        sh = sh_ref[...]
        for s in range(0, rc, ROW_SUB):
            x = xn_ref[s:s + ROW_SUB, :]
            h_ref[rows(base, s, ROW_SUB), :] = (x * _rms_scale(x) * a + sh).astype(BF16)

    def finish(base):
        gg = gt_ref[...] * gpost_ref[...]
        for s in range(0, rc, ROW_SUB):
            y = acc_ref[rows(base, s, ROW_SUB), :]
            out_ref[s:s + ROW_SUB, :] = xr_ref[s:s + ROW_SUB, :] + y * _rms_scale(y) * gg

    def clear(base):
        acc_ref[rows(base, 0, rc), :] = jnp.zeros((rc, d), F32)

    @pl.when(i == 0)
    def _():
        normalise(tile_base(0))
        clear(tile_base(0))
        clear(tile_base(1))

    @pl.when((i >= 1) & (i <= n_tiles))
    def _():
        def side_work():
            finish(tile_base(i))
            clear(tile_base(i))
            normalise(tile_base(i))
            for src, dst in zip(cast_in, cast_out):
                dst[...] = src[...].astype(BF16)

        step_fn(i - 1, j, w_refs, h_ref, acc_ref, pl.ds(tile_base(i - 1), tm), extra, side_work)

    @pl.when(i == n_tiles + 1)
    def _():
        finish(tile_base(n_tiles - 1))


def _sublayer(x, shift, scale, gate, g_pre, g_post, weights, weight_specs, step_fn, casts,
              *, seq, tm, nj, extra_scratch, name):
    t, d = x.shape
    n = t // tm
    rc = tm // nj
    tps = seq // tm
    steps = n * nj

    def norm_tile(i):
        return jnp.minimum(i, n - 1)

    def done_tile(i):
        return jnp.clip(i - 2, 0, n - 1)

    def done_map(i, j):
        return (done_tile(i) * nj + jnp.where(i < 2, 0, j), 0)

    def wj(i, j):
        return jnp.where(i == 0, 0, jnp.where(i == n + 1, nj - 1, j))

    def cast_step(i, j):
        return jnp.clip((i - 1) * nj + j, 0, steps - 1)

    pre_mod = pl.BlockSpec((None, 1, d), lambda i, j: (norm_tile(i) // tps, 0, 0))
    post_mod = pl.BlockSpec((None, 1, d), lambda i, j: (done_tile(i) // tps, 0, 0))
    vec = pl.BlockSpec((1, d), lambda i, j: (0, 0))
    cast_in_specs, cast_out_specs, cast_shapes = [], [], []
    for w, layer in casts:
        _, r, c = w.shape
        assert r % (steps * 2 * SUBLANES) == 0, "cast row blocks must be whole bf16 tiles"
        cast_in_specs.append(pl.BlockSpec(
            (None, r // steps, c), lambda i, j, layer=layer: (layer, cast_step(i, j), 0)))
        cast_out_specs.append(pl.BlockSpec((r // steps, c), lambda i, j: (cast_step(i, j), 0)))
        cast_shapes.append(jax.ShapeDtypeStruct((r, c), BF16))

    outs = pl.pallas_call(
        functools.partial(_sublayer_kernel, n_tiles=n, n_w=len(weights), n_cast=len(casts),
                          step_fn=step_fn),
        grid=(n + 2, nj),
        in_specs=[
            pl.BlockSpec((rc, d), lambda i, j: (norm_tile(i) * nj + j, 0)),
            pl.BlockSpec((rc, d), done_map),
            pre_mod, pre_mod, post_mod, vec, vec,
            *weight_specs(wj),
            *cast_in_specs,
        ],
        out_specs=[pl.BlockSpec((rc, d), done_map), *cast_out_specs],
        out_shape=[jax.ShapeDtypeStruct((t, d), F32), *cast_shapes],
        scratch_shapes=[
            pltpu.VMEM((2 * tm, d), BF16),
            pltpu.VMEM((2 * tm, d), F32),
            *extra_scratch,
        ],
        compiler_params=_params(("arbitrary", "arbitrary")),
        name=name,
    )(x, x, shift, scale, gate, g_pre, g_post, *weights, *[w for w, _ in casts])
    return outs[0], outs[1:]


def _mixer_step(tile, j, w_refs, h_ref, acc_ref, tile_rows, extra, side_work, *, tiles_per_seq):
    cw_ref, wb_ref, wc_ref, wu_ref, wo_ref = w_refs
    zbuf_ref, carry_ref = extra
    tm = tile_rows.size
    cg = _dot(h_ref[tile_rows, :], wc_ref[...])
    side_work()
    h = h_ref[tile_rows, :]
    z = cg * _dot(h, wu_ref[...])

    seq_start = (tile % tiles_per_seq) == 0
    zbuf_ref[0:SUBLANES, :] = jnp.where(seq_start, 0.0, carry_ref[j])
    zbuf_ref[SUBLANES:, :] = z
    carry_ref[j] = z[tm - SUBLANES:, :]
    cw = cw_ref[...]
    zc = (cw[2:3, :] * z
          + cw[1:2, :] * zbuf_ref[SUBLANES - 1:SUBLANES - 1 + tm, :]
          + cw[0:1, :] * zbuf_ref[SUBLANES - 2:SUBLANES - 2 + tm, :])
    bg = _dot(h_ref[tile_rows, :], wb_ref[...])
    acc_ref[tile_rows, :] += _dot((bg * zc).astype(BF16), wo_ref[...])


def _mixer(x, shift, scale, gate, g_pre, g_post, conv_w, w_in, w_out, casts, seq,
           tm=512, tn=512):
    d = x.shape[1]
    nj = d // tn

    def weight_specs(wj):
        return [
            pl.BlockSpec((3, tn), lambda i, j: (0, wj(i, j))),
            pl.BlockSpec((d, tn), lambda i, j: (0, wj(i, j))),
            pl.BlockSpec((d, tn), lambda i, j: (0, nj + wj(i, j))),
            pl.BlockSpec((d, tn), lambda i, j: (0, 2 * nj + wj(i, j))),
            pl.BlockSpec((tn, d), lambda i, j: (wj(i, j), 0)),
        ]

    return _sublayer(
        x, shift, scale, gate, g_pre, g_post, [conv_w, w_in, w_in, w_in, w_out], weight_specs,
        functools.partial(_mixer_step, tiles_per_seq=seq // tm), casts,
        seq=seq, tm=tm, nj=nj,
        extra_scratch=[pltpu.VMEM((tm + SUBLANES, tn), F32),
                       pltpu.VMEM((nj, SUBLANES, tn), F32)],
        name="conv_mixer")


MLP_SUB = 512


def _mlp_step(tile, j, w_refs, h_ref, acc_ref, tile_rows, extra, side_work):
    wup_ref, wdn_ref = w_refs
    for c in range(0, wup_ref.shape[1], MLP_SUB):
        if c == MLP_SUB:
            side_work()
        a = jnp.maximum(_dot(h_ref[tile_rows, :], wup_ref[:, c:c + MLP_SUB]), 0.0)
        acc_ref[tile_rows, :] += _dot((a * a).astype(BF16), wdn_ref[c:c + MLP_SUB, :])


def _mlp(x, shift, scale, gate, g_pre, g_post, w_up, w_down, casts, seq, tm=1024, tf=1024):
    d = x.shape[1]
    dff = w_up.shape[1]

    def weight_specs(wj):
        return [
            pl.BlockSpec((d, tf), lambda i, j: (0, wj(i, j))),
            pl.BlockSpec((tf, d), lambda i, j: (wj(i, j), 0)),
        ]

    return _sublayer(
        x, shift, scale, gate, g_pre, g_post, [w_up, w_down], weight_specs, _mlp_step, casts,
        seq=seq, tm=tm, nj=dff // tf, extra_scratch=[], name="relu2_mlp")


ROW_PARTS = 4


def _rope(t, rows, cos_ref, s1_ref, s2_ref):
    n = t.shape[1]
    reps = n // LANES
    cos = jnp.tile(cos_ref[rows, :], (1, reps))
    s1 = jnp.tile(s1_ref[rows, :], (1, reps))
    s2 = jnp.tile(s2_ref[rows, :], (1, reps))
    half = ROT_DIM // 2
    return (t * cos + pltpu.roll(t, n - half, 1) * s1 + pltpu.roll(t, half, 1) * s2)


def _dup_heads(t):
    lo = lax.broadcasted_iota(jnp.int32, (t.shape[0], LANES), 1) < HEAD_DIM
    out = []
    for m in range(t.shape[1] // LANES):
        slab = t[:, m * LANES:(m + 1) * LANES]
        swapped = pltpu.roll(slab, HEAD_DIM, 1)
        out += [jnp.where(lo, slab, swapped), jnp.where(lo, swapped, slab)]
    return jnp.concatenate(out, axis=1)


Q_COLS = 512


def _qkv_kernel(x_ref, shq_ref, scq_ref, gq_ref, shk_ref, sck_ref, gk_ref,
                wq_ref, bq_ref, wkv_ref, bkv_ref,
                cos_ref, s1_ref, s2_ref, q_ref, k_ref, v_ref):
    aq = gq_ref[...] * (1.0 + scq_ref[...])
    ak = gk_ref[...] * (1.0 + sck_ref[...])
    part = x_ref.shape[0] // ROW_PARTS
    for r in range(0, x_ref.shape[0], part):
        rows = slice(r, r + part)
        x = x_ref[rows, :]
        xn = x * _rms_scale(x)
        hq = (xn * aq + shq_ref[...]).astype(BF16)
        hk = (xn * ak + shk_ref[...]).astype(BF16)
        for c in range(0, wq_ref.shape[1], Q_COLS):
            q = _dot(hq, wq_ref[:, c:c + Q_COLS]) + bq_ref[:, c:c + Q_COLS]
            q_ref[rows, c:c + Q_COLS] = (_rope(q, rows, cos_ref, s1_ref, s2_ref)
                                         * (HEAD_DIM ** -0.5)).astype(BF16)
        kv = _dot(hk, wkv_ref[...]) + bkv_ref[...]
        nk = kv.shape[1] // 2
        k_ref[rows, :] = _dup_heads(_rope(kv[:, :nk], rows, cos_ref, s1_ref, s2_ref)).astype(BF16)
        v_ref[rows, :] = _dup_heads(kv[:, nk:]).astype(BF16)


def _qkv(x, shq, scq, gq, shk, sck, gk, wq, bq, wkv, bkv, cos, s1, s2, seq, tm=1024):
    t, d = x.shape
    nq = wq.shape[1]
    nk = wkv.shape[1]
    tps = seq // tm
    mod_spec = pl.BlockSpec((None, 1, d), lambda i: (i // tps, 0, 0))
    vec_spec = pl.BlockSpec((1, d), lambda i: (0, 0))
    tab_spec = pl.BlockSpec((tm, LANES), lambda i: (i % tps, 0))

    def full(a):
        return pl.BlockSpec(a.shape, lambda i: (0, 0))

    return pl.pallas_call(
        _qkv_kernel,
        grid=(t // tm,),
        in_specs=[
            pl.BlockSpec((tm, d), lambda i: (i, 0)),
            mod_spec, mod_spec, vec_spec, mod_spec, mod_spec, vec_spec,
            full(wq), full(bq), full(wkv), full(bkv),
            tab_spec, tab_spec, tab_spec,
        ],
        out_specs=[
            pl.BlockSpec((tm, nq), lambda i: (i, 0)),
            pl.BlockSpec((tm, nk), lambda i: (i, 0)),
            pl.BlockSpec((tm, nk), lambda i: (i, 0)),
        ],
        out_shape=[
            jax.ShapeDtypeStruct((t, nq), BF16),
            jax.ShapeDtypeStruct((t, nk), BF16),
            jax.ShapeDtypeStruct((t, nk), BF16),
        ],
        compiler_params=_params(("parallel",)),
        name="qkv_proj",
    )(x, shq, scq, gq, shk, sck, gk, wq, bq, wkv, bkv, cos, s1, s2)


def _attn_kernel(sinks_ref, q_ref, kp_ref, kc_ref, vp_ref, vc_ref, o_ref, *, q_blocks):
    n = pl.program_id(1)
    blk = ATT_BLOCK
    n_kv = kp_ref.shape[1] // LANES
    pairs_per_kv = KV_GROUP // 2

    qi = lax.broadcasted_iota(jnp.int32, (2 * blk, LANES), 0) & (blk - 1)
    kc = lax.broadcasted_iota(jnp.int32, (2 * blk, LANES), 1)
    from_prev = kc > qi
    lo = lax.broadcasted_iota(jnp.int32, (blk, LANES), 1) < HEAD_DIM
    top = lax.broadcasted_iota(jnp.int32, (2 * blk, 1), 0) < blk
    ones = jnp.ones((blk, LANES), BF16)
    no_prev_bias = jnp.where(n == 0, -jnp.inf, 0.0)

    for g in range(n_kv):
        cols = slice(g * LANES, (g + 1) * LANES)
        for b in range(q_blocks):
            cur = slice(b * blk, (b + 1) * blk)
            if b == 0:
                k_prev, v_prev = kp_ref[:, cols], vp_ref[:, cols]
            else:
                prv = slice((b - 1) * blk, b * blk)
                k_prev, v_prev = kc_ref[prv, cols], vc_ref[prv, cols]
            kband = jnp.concatenate([k_prev, kc_ref[cur, cols]], axis=0)
            vext = jnp.concatenate(
                [jnp.concatenate([v_prev, ones], axis=1),
                 jnp.concatenate([vc_ref[cur, cols], ones], axis=1)], axis=0)
            for pp in range(pairs_per_kv):
                p = g * pairs_per_kv + pp
                pcols = slice(p * LANES, (p + 1) * LANES)
                qp = q_ref[cur, pcols]
                zero = jnp.zeros_like(qp)
                q2 = jnp.concatenate([jnp.where(lo, qp, zero), jnp.where(lo, zero, qp)], axis=0)
                s = lax.dot_general(q2, kband, (((1,), (1,)), ((), ())),
                                    preferred_element_type=F32)
                s_prev = s[:, :LANES]
                if b == 0:
                    s_prev = s_prev + no_prev_bias
                sc = jnp.where(from_prev, s_prev, s[:, LANES:])
                sink = jnp.where(top, sinks_ref[2 * p], sinks_ref[2 * p + 1])
                m = jnp.maximum(jnp.max(sc, axis=-1, keepdims=True), sink)
                e = jnp.exp(sc - m).astype(BF16)
                ez = jnp.zeros_like(e)
                pmat = jnp.concatenate([jnp.where(from_prev, e, ez),
                                        jnp.where(from_prev, ez, e)], axis=1)
                pv = _dot(pmat, vext)
                den = pv[:, LANES:] + jnp.exp(sink - m)
                num = jnp.where(lo, pv[:blk, :LANES], pv[blk:, :LANES])
                o_ref[cur, pcols] = (num / jnp.where(lo, den[:blk], den[blk:])).astype(BF16)


def _attention(sinks, q, kd, vd, seq, q_blocks=8):
    t, nq = q.shape
    nk = kd.shape[1]
    tq = q_blocks * ATT_BLOCK
    steps = seq // tq
    batch = t // seq
    cur = lambda b, n: (b * steps + n, 0)
    prev = lambda b, n: ((b * steps + n) * q_blocks - jnp.where(n == 0, 0, 1), 0)
    return pl.pallas_call(
        functools.partial(_attn_kernel, q_blocks=q_blocks),
        grid=(batch, steps),
        in_specs=[
            pl.BlockSpec(memory_space=pltpu.SMEM),
            pl.BlockSpec((tq, nq), cur),
            pl.BlockSpec((ATT_BLOCK, nk), prev),
            pl.BlockSpec((tq, nk), cur),
            pl.BlockSpec((ATT_BLOCK, nk), prev),
            pl.BlockSpec((tq, nk), cur),
        ],
        out_specs=pl.BlockSpec((tq, nq), cur),
        out_shape=jax.ShapeDtypeStruct((t, nq), BF16),
        compiler_params=_params(("parallel", "parallel")),
        name="swa_attention",
    )(sinks, q, kd, kd, vd, vd)


def _oproj_kernel(x_ref, o_ref, gt_ref, gpost_ref, wo_ref, bo_ref, out_ref):
    gg = gt_ref[...] * gpost_ref[...]
    part = x_ref.shape[0] // ROW_PARTS
    for r in range(0, x_ref.shape[0], part):
        rows = slice(r, r + part)
        y = _dot(o_ref[rows, :], wo_ref[...]) + bo_ref[...]
        out_ref[rows, :] = x_ref[rows, :] + y * _rms_scale(y) * gg


def _oproj(x, o, gate, g_post, w_o, b_o, seq, tm=1024):
    t, d = x.shape
    nq = o.shape[1]
    tps = seq // tm
    return pl.pallas_call(
        _oproj_kernel,
        grid=(t // tm,),
        in_specs=[
            pl.BlockSpec((tm, d), lambda i: (i, 0)),
            pl.BlockSpec((tm, nq), lambda i: (i, 0)),
            pl.BlockSpec((None, 1, d), lambda i: (i // tps, 0, 0)),
            pl.BlockSpec((1, d), lambda i: (0, 0)),
            pl.BlockSpec((nq, d), lambda i: (0, 0)),
            pl.BlockSpec((1, d), lambda i: (0, 0)),
        ],
        out_specs=pl.BlockSpec((tm, d), lambda i: (i, 0)),
        out_shape=jax.ShapeDtypeStruct((t, d), F32),
        compiler_params=_params(("parallel",)),
        name="attn_out_proj",
    )(x, o, gate, g_post, w_o, b_o)


def _rope_tables(seq):
    half = ROT_DIM // 2
    inv = ROPE_THETA ** (-jnp.arange(0, ROT_DIM, 2, dtype=F32) / ROT_DIM)
    ang = jnp.arange(seq, dtype=F32)[:, None] * inv[None, :]
    cos, sin = jnp.cos(ang), jnp.sin(ang)
    ones = jnp.ones((seq, HEAD_DIM - ROT_DIM), F32)
    zeros = jnp.zeros((seq, HEAD_DIM - ROT_DIM), F32)
    zh = jnp.zeros((seq, half), F32)
    c_tab = jnp.concatenate([cos, cos, ones], axis=1)
    s1_tab = jnp.concatenate([-sin, zh, zeros], axis=1)
    s2_tab = jnp.concatenate([zh, sin, zeros], axis=1)
    rep = LANES // HEAD_DIM
    return tuple(jnp.tile(a, (1, rep)) for a in (c_tab, s1_tab, s2_tab))


def kernel(x, c, ada_w, ada_b, norm_pre, norm_post, conv_w_in, conv_w, conv_w_out,
           kv_ada_w, kv_ada_b, kv_norm, w_kv, b_kv, w_q, b_q, sinks, w_o, b_o,
           mlp_up, mlp_down):
    batch, seq, d = x.shape
    depth = ada_w.shape[0]
    n_self = conv_w_in.shape[0]
    t = batch * seq
    assert n_self == 1 and depth == 2, "one conv-mixer layer then one shared-KV attention layer"

    c_pad = jnp.pad(c, ((0, SUBLANES - batch), (0, 0)))
    mods = _ada(c_pad, ada_w.reshape(depth * 2, d, 3 * d), ada_b.reshape(depth * 2, 1, 3 * d))
    mods = mods[:, :batch].reshape(depth, 2, batch, 3, 1, d)
    kv_mods = _ada(c_pad, kv_ada_w[None], kv_ada_b[None, None])
    kv_mods = kv_mods[0, :batch].reshape(batch, 2, 1, d)

    def mod(l, sub):
        return [mods[l, sub, :, which] for which in range(3)]

    xf = x.reshape(t, d)

    xf, (up0, down0) = _mixer(
        xf, *mod(0, 0), norm_pre[0, 0:1], norm_post[0, 0:1], conv_w[0],
        conv_w_in[0].astype(BF16), conv_w_out[0].astype(BF16),
        [(mlp_up, 0), (mlp_down, 0)], seq)
    xf, (wq_bf, wo_bf, up1, down1) = _mlp(
        xf, *mod(0, 1), norm_pre[0, 1:2], norm_post[0, 1:2], up0, down0,
        [(w_q, 0), (w_o, 0), (mlp_up, 1), (mlp_down, 1)], seq)

    cos, s1, s2 = _rope_tables(seq)
    shift, scale, gate = mod(1, 0)
    q, kd, vd = _qkv(xf, shift, scale, norm_pre[1, 0:1],
                     kv_mods[:, 0], kv_mods[:, 1], kv_norm[None],
                     wq_bf, b_q[0][None], w_kv.astype(BF16), b_kv[None], cos, s1, s2, seq)
    o = _attention(sinks[0], q, kd, vd, seq)
    xf = _oproj(xf, o, gate, norm_post[1, 0:1], wo_bf, b_o[0][None], seq)
    xf, _ = _mlp(xf, *mod(1, 1), norm_pre[1, 1:2], norm_post[1, 1:2], up1, down1, [], seq)
    return xf.reshape(batch, seq, d)
```

```python
import functools

import jax
import jax.numpy as jnp
from jax import lax
from jax.experimental import pallas as pl
from jax.experimental.pallas import tpu as pltpu

EPS = 1e-6
HEAD_DIM = 64
KV_GROUP = 8
ROT_DIM = HEAD_DIM // 4
ROPE_THETA = 500000.0
ATT_BLOCK = 128
LANES = 128
SUBLANES = 8
VMEM_LIMIT = 56 * 1024 * 1024
ROW_SUB = 32

BF16 = jnp.bfloat16
F32 = jnp.float32


def _dot(a, b):
    return jnp.dot(a, b, preferred_element_type=F32)


def _rms_scale(x):
    return lax.rsqrt(jnp.mean(x * x, axis=-1, keepdims=True) + EPS)


def _params(sem):
    return pltpu.CompilerParams(dimension_semantics=sem, vmem_limit_bytes=VMEM_LIMIT)


def _ada_kernel(*refs, n_cast):
    c_ref, w_ref, b_ref = refs[:3]
    cast_in = refs[3:3 + n_cast]
    o_ref = refs[3 + n_cast]
    cast_out = refs[4 + n_cast:]
    c = c_ref[...]
    c_act = c / (1.0 + jnp.exp(-c))
    o_ref[...] = _dot(c_act.astype(BF16), w_ref[...].astype(BF16)) + b_ref[...]
    for src, dst in zip(cast_in, cast_out):
        dst[...] = src[...].astype(BF16)


def _ada(c_pad, w, b, casts=(), tn=2048):
    g, d, n = w.shape
    nb = n // tn
    steps = g * nb
    cast_in_specs, cast_out_specs, cast_shapes = [], [], []
    for cw, layer in casts:
        _, r, c = cw.shape
        assert r % (steps * 2 * SUBLANES) == 0, "cast row blocks must be whole bf16 tiles"
        cast_in_specs.append(pl.BlockSpec(
            (None, r // steps, c), lambda a, j, layer=layer: (layer, a * nb + j, 0)))
        cast_out_specs.append(pl.BlockSpec((r // steps, c), lambda a, j: (a * nb + j, 0)))
        cast_shapes.append(jax.ShapeDtypeStruct((r, c), BF16))
    outs = pl.pallas_call(
        functools.partial(_ada_kernel, n_cast=len(casts)),
        grid=(g, nb),
        in_specs=[
            pl.BlockSpec((SUBLANES, d), lambda a, j: (0, 0)),
            pl.BlockSpec((None, d, tn), lambda a, j: (a, 0, j)),
            pl.BlockSpec((None, 1, tn), lambda a, j: (a, 0, j)),
            *cast_in_specs,
        ],
        out_specs=[pl.BlockSpec((None, SUBLANES, tn), lambda a, j: (a, 0, j)), *cast_out_specs],
        out_shape=[jax.ShapeDtypeStruct((g, SUBLANES, n), F32), *cast_shapes],
        compiler_params=_params(("parallel", "parallel")),
        name="ada_mod",
    )(c_pad, w, b, *[cw for cw, _ in casts])
    return outs[0], outs[1:]


def _sublayer_kernel(*refs, n_tiles, n_w, n_cast, step_fn):
    xn_ref, xr_ref, sh_ref, sc_ref, gt_ref, gpre_ref, gpost_ref = refs[:7]
    k = 7
    w_refs = refs[k:k + n_w]
    k += n_w
    cast_in = refs[k:k + n_cast]
    k += n_cast
    out_ref = refs[k]
    cast_out = refs[k + 1:k + 1 + n_cast]
    k += 1 + n_cast
    h_ref, acc_ref = refs[k:k + 2]
    extra = refs[k + 2:]

    i = pl.program_id(0)
    j = pl.program_id(1)
    rc, d = xn_ref.shape
    tm = h_ref.shape[0] // 2
    row0 = j * rc

    def tile_base(t):
        if isinstance(t, int):
            return (t % 2) * tm
        return pl.multiple_of((t % 2) * tm, tm)

    def rows(base, s, size):
        return pl.ds(pl.multiple_of(base + row0 + s, size), size)

    def normalise(base):
        a = gpre_ref[...] * (1.0 + sc_ref[...])
        sh = sh_ref[...]
        for s in range(0, rc, ROW_SUB):
            x = xn_ref[s:s + ROW_SUB, :]
            h_ref[rows(base, s, ROW_SUB), :] = (x * _rms_scale(x) * a + sh).astype(BF16)

    def finish(base):
        gg = gt_ref[...] * gpost_ref[...]
        for s in range(0, rc, ROW_SUB):
            y = acc_ref[rows(base, s, ROW_SUB), :]
            out_ref[s:s + ROW_SUB, :] = xr_ref[s:s + ROW_SUB, :] + y * _rms_scale(y) * gg

    def clear(base):
        acc_ref[rows(base, 0, rc), :] = jnp.zeros((rc, d), F32)

    @pl.when(i == 0)
    def _():
        normalise(tile_base(0))
        clear(tile_base(0))
        clear(tile_base(1))

    @pl.when((i >= 1) & (i <= n_tiles))
    def _():
        def side_work():
            finish(tile_base(i))
            clear(tile_base(i))
            normalise(tile_base(i))
            for src, dst in zip(cast_in, cast_out):
                dst[...] = src[...].astype(BF16)

        step_fn(i - 1, j, w_refs, h_ref, acc_ref, pl.ds(tile_base(i - 1), tm), extra, side_work)

    @pl.when(i == n_tiles + 1)
    def _():
        finish(tile_base(n_tiles - 1))


def _sublayer(x, shift, scale, gate, g_pre, g_post, weights, weight_specs, step_fn, casts,
              *, seq, tm, nj, extra_scratch, name):
    t, d = x.shape
    n = t // tm
    rc = tm // nj
    tps = seq // tm
    steps = n * nj

    def norm_tile(i):
        return jnp.minimum(i, n - 1)

    def done_tile(i):
        return jnp.clip(i - 2, 0, n - 1)

    def done_map(i, j):
        return (done_tile(i) * nj + jnp.where(i < 2, 0, j), 0)

    def wj(i, j):
        return jnp.where(i == 0, 0, jnp.where(i == n + 1, nj - 1, j))

    def cast_step(i, j):
        return jnp.clip((i - 1) * nj + j, 0, steps - 1)

    pre_mod = pl.BlockSpec((None, 1, d), lambda i, j: (norm_tile(i) // tps, 0, 0))
    post_mod = pl.BlockSpec((None, 1, d), lambda i, j: (done_tile(i) // tps, 0, 0))
    vec = pl.BlockSpec((1, d), lambda i, j: (0, 0))
    cast_in_specs, cast_out_specs, cast_shapes = [], [], []
    for w, layer in casts:
        _, r, c = w.shape
        assert r % (steps * 2 * SUBLANES) == 0, "cast row blocks must be whole bf16 tiles"
        cast_in_specs.append(pl.BlockSpec(
            (None, r // steps, c), lambda i, j, layer=layer: (layer, cast_step(i, j), 0)))
        cast_out_specs.append(pl.BlockSpec((r // steps, c), lambda i, j: (cast_step(i, j), 0)))
        cast_shapes.append(jax.ShapeDtypeStruct((r, c), BF16))

    outs = pl.pallas_call(
        functools.partial(_sublayer_kernel, n_tiles=n, n_w=len(weights), n_cast=len(casts),
                          step_fn=step_fn),
        grid=(n + 2, nj),
        in_specs=[
            pl.BlockSpec((rc, d), lambda i, j: (norm_tile(i) * nj + j, 0)),
            pl.BlockSpec((rc, d), done_map),
            pre_mod, pre_mod, post_mod, vec, vec,
            *weight_specs(wj),
            *cast_in_specs,
        ],
        out_specs=[pl.BlockSpec((rc, d), done_map), *cast_out_specs],
        out_shape=[jax.ShapeDtypeStruct((t, d), F32), *cast_shapes],
        scratch_shapes=[
            pltpu.VMEM((2 * tm, d), BF16),
            pltpu.VMEM((2 * tm, d), F32),
            *extra_scratch,
        ],
        compiler_params=_params(("arbitrary", "arbitrary")),
        name=name,
    )(x, x, shift, scale, gate, g_pre, g_post, *weights, *[w for w, _ in casts])
    return outs[0], outs[1:]


def _mixer_step(tile, j, w_refs, h_ref, acc_ref, tile_rows, extra, side_work, *, tiles_per_seq):
    cw_ref, wb_ref, wc_ref, wu_ref, wo_ref = w_refs
    zbuf_ref, carry_ref = extra
    tm = tile_rows.size
    cg = _dot(h_ref[tile_rows, :], wc_ref[...])
    side_work()
    h = h_ref[tile_rows, :]
    z = cg * _dot(h, wu_ref[...])

    seq_start = (tile % tiles_per_seq) == 0
    zbuf_ref[0:SUBLANES, :] = jnp.where(seq_start, 0.0, carry_ref[j])
    zbuf_ref[SUBLANES:, :] = z
    carry_ref[j] = z[tm - SUBLANES:, :]
    cw = cw_ref[...]
    zc = (cw[2:3, :] * z
          + cw[1:2, :] * zbuf_ref[SUBLANES - 1:SUBLANES - 1 + tm, :]
          + cw[0:1, :] * zbuf_ref[SUBLANES - 2:SUBLANES - 2 + tm, :])
    bg = _dot(h_ref[tile_rows, :], wb_ref[...])
    acc_ref[tile_rows, :] += _dot((bg * zc).astype(BF16), wo_ref[...])


def _mixer(x, shift, scale, gate, g_pre, g_post, conv_w, w_in, w_out, casts, seq,
           tm=512, tn=512):
    d = x.shape[1]
    nj = d // tn

    def weight_specs(wj):
        return [
            pl.BlockSpec((3, tn), lambda i, j: (0, wj(i, j))),
            pl.BlockSpec((d, tn), lambda i, j: (0, wj(i, j))),
            pl.BlockSpec((d, tn), lambda i, j: (0, nj + wj(i, j))),
            pl.BlockSpec((d, tn), lambda i, j: (0, 2 * nj + wj(i, j))),
            pl.BlockSpec((tn, d), lambda i, j: (wj(i, j), 0)),
        ]

    return _sublayer(
        x, shift, scale, gate, g_pre, g_post, [conv_w, w_in, w_in, w_in, w_out], weight_specs,
        functools.partial(_mixer_step, tiles_per_seq=seq // tm), casts,
        seq=seq, tm=tm, nj=nj,
        extra_scratch=[pltpu.VMEM((tm + SUBLANES, tn), F32),
                       pltpu.VMEM((nj, SUBLANES, tn), F32)],
        name="conv_mixer")


MLP_SUB = 512


def _mlp_step(tile, j, w_refs, h_ref, acc_ref, tile_rows, extra, side_work):
    wup_ref, wdn_ref = w_refs
    for c in range(0, wup_ref.shape[1], MLP_SUB):
        if c == MLP_SUB:
            side_work()
        a = jnp.maximum(_dot(h_ref[tile_rows, :], wup_ref[:, c:c + MLP_SUB]), 0.0)
        acc_ref[tile_rows, :] += _dot((a * a).astype(BF16), wdn_ref[c:c + MLP_SUB, :])


def _mlp(x, shift, scale, gate, g_pre, g_post, w_up, w_down, casts, seq, tm=1024, tf=1024):
    d = x.shape[1]
    dff = w_up.shape[1]

    def weight_specs(wj):
        return [
            pl.BlockSpec((d, tf), lambda i, j: (0, wj(i, j))),
            pl.BlockSpec((tf, d), lambda i, j: (wj(i, j), 0)),
        ]

    return _sublayer(
        x, shift, scale, gate, g_pre, g_post, [w_up, w_down], weight_specs, _mlp_step, casts,
        seq=seq, tm=tm, nj=dff // tf, extra_scratch=[], name="relu2_mlp")


ROW_PARTS = 4


def _rope(t, rows, cos_ref, s1_ref, s2_ref):
    n = t.shape[1]
    reps = n // LANES
    cos = jnp.tile(cos_ref[rows, :], (1, reps))
    s1 = jnp.tile(s1_ref[rows, :], (1, reps))
    s2 = jnp.tile(s2_ref[rows, :], (1, reps))
    half = ROT_DIM // 2
    return (t * cos + pltpu.roll(t, n - half, 1) * s1 + pltpu.roll(t, half, 1) * s2)


def _dup_heads(t):
    lo = lax.broadcasted_iota(jnp.int32, (t.shape[0], LANES), 1) < HEAD_DIM
    out = []
    for m in range(t.shape[1] // LANES):
        slab = t[:, m * LANES:(m + 1) * LANES]
        swapped = pltpu.roll(slab, HEAD_DIM, 1)
        out += [jnp.where(lo, slab, swapped), jnp.where(lo, swapped, slab)]
    return jnp.concatenate(out, axis=1)


Q_COLS = 512


def _qkv_kernel(x_ref, shq_ref, scq_ref, gq_ref, shk_ref, sck_ref, gk_ref,
                wq_ref, bq_ref, wkv_ref, bkv_ref,
                cos_ref, s1_ref, s2_ref, q_ref, k_ref, v_ref):
    aq = gq_ref[...] * (1.0 + scq_ref[...])
    ak = gk_ref[...] * (1.0 + sck_ref[...])
    part = x_ref.shape[0] // ROW_PARTS
    for r in range(0, x_ref.shape[0], part):
        rows = slice(r, r + part)
        x = x_ref[rows, :]
        xn = x * _rms_scale(x)
        hq = (xn * aq + shq_ref[...]).astype(BF16)
        hk = (xn * ak + shk_ref[...]).astype(BF16)
        for c in range(0, wq_ref.shape[1], Q_COLS):
            q = _dot(hq, wq_ref[:, c:c + Q_COLS]) + bq_ref[:, c:c + Q_COLS]
            q_ref[rows, c:c + Q_COLS] = (_rope(q, rows, cos_ref, s1_ref, s2_ref)
                                         * (HEAD_DIM ** -0.5)).astype(BF16)
        kv = _dot(hk, wkv_ref[...]) + bkv_ref[...]
        nk = kv.shape[1] // 2
        k_ref[rows, :] = _dup_heads(_rope(kv[:, :nk], rows, cos_ref, s1_ref, s2_ref)).astype(BF16)
        v_ref[rows, :] = _dup_heads(kv[:, nk:]).astype(BF16)


def _qkv(x, shq, scq, gq, shk, sck, gk, wq, bq, wkv, bkv, cos, s1, s2, seq, tm=1024):
    t, d = x.shape
    nq = wq.shape[1]
    nk = wkv.shape[1]
    tps = seq // tm
    mod_spec = pl.BlockSpec((None, 1, d), lambda i: (i // tps, 0, 0))
    vec_spec = pl.BlockSpec((1, d), lambda i: (0, 0))
    tab_spec = pl.BlockSpec((tm, LANES), lambda i: (i % tps, 0))

    def full(a):
        return pl.BlockSpec(a.shape, lambda i: (0, 0))

    return pl.pallas_call(
        _qkv_kernel,
        grid=(t // tm,),
        in_specs=[
            pl.BlockSpec((tm, d), lambda i: (i, 0)),
            mod_spec, mod_spec, vec_spec, mod_spec, mod_spec, vec_spec,
            full(wq), full(bq), full(wkv), full(bkv),
            tab_spec, tab_spec, tab_spec,
        ],
        out_specs=[
            pl.BlockSpec((tm, nq), lambda i: (i, 0)),
            pl.BlockSpec((tm, nk), lambda i: (i, 0)),
            pl.BlockSpec((tm, nk), lambda i: (i, 0)),
        ],
        out_shape=[
            jax.ShapeDtypeStruct((t, nq), BF16),
            jax.ShapeDtypeStruct((t, nk), BF16),
            jax.ShapeDtypeStruct((t, nk), BF16),
        ],
        compiler_params=_params(("parallel",)),
        name="qkv_proj",
    )(x, shq, scq, gq, shk, sck, gk, wq, bq, wkv, bkv, cos, s1, s2)


def _attn_kernel(sinks_ref, q_ref, kp_ref, kc_ref, vp_ref, vc_ref, o_ref, *, q_blocks):
    n = pl.program_id(1)
    blk = ATT_BLOCK
    n_kv = kp_ref.shape[1] // LANES
    pairs_per_kv = KV_GROUP // 2

    qi = lax.broadcasted_iota(jnp.int32, (2 * blk, LANES), 0) & (blk - 1)
    kc = lax.broadcasted_iota(jnp.int32, (2 * blk, LANES), 1)
    from_prev = kc > qi
    lo = lax.broadcasted_iota(jnp.int32, (blk, LANES), 1) < HEAD_DIM
    top = lax.broadcasted_iota(jnp.int32, (2 * blk, 1), 0) < blk
    ones = jnp.ones((blk, LANES), BF16)
    no_prev_bias = jnp.where(n == 0, -jnp.inf, 0.0)

    for g in range(n_kv):
        cols = slice(g * LANES, (g + 1) * LANES)
        for b in range(q_blocks):
            cur = slice(b * blk, (b + 1) * blk)
            if b == 0:
                k_prev, v_prev = kp_ref[:, cols], vp_ref[:, cols]
            else:
                prv = slice((b - 1) * blk, b * blk)
                k_prev, v_prev = kc_ref[prv, cols], vc_ref[prv, cols]
            kband = jnp.concatenate([k_prev, kc_ref[cur, cols]], axis=0)
            vext = jnp.concatenate(
                [jnp.concatenate([v_prev, ones], axis=1),
                 jnp.concatenate([vc_ref[cur, cols], ones], axis=1)], axis=0)
            for pp in range(pairs_per_kv):
                p = g * pairs_per_kv + pp
                pcols = slice(p * LANES, (p + 1) * LANES)
                qp = q_ref[cur, pcols]
                zero = jnp.zeros_like(qp)
                q2 = jnp.concatenate([jnp.where(lo, qp, zero), jnp.where(lo, zero, qp)], axis=0)
                s = lax.dot_general(q2, kband, (((1,), (1,)), ((), ())),
                                    preferred_element_type=F32)
                s_prev = s[:, :LANES]
                if b == 0:
                    s_prev = s_prev + no_prev_bias
                sc = jnp.where(from_prev, s_prev, s[:, LANES:])
                sink = jnp.where(top, sinks_ref[2 * p], sinks_ref[2 * p + 1])
                m = jnp.maximum(jnp.max(sc, axis=-1, keepdims=True), sink)
                e = jnp.exp(sc - m).astype(BF16)
                ez = jnp.zeros_like(e)
                pmat = jnp.concatenate([jnp.where(from_prev, e, ez),
                                        jnp.where(from_prev, ez, e)], axis=1)
                pv = _dot(pmat, vext)
                den = pv[:, LANES:] + jnp.exp(sink - m)
                num = jnp.where(lo, pv[:blk, :LANES], pv[blk:, :LANES])
                o_ref[cur, pcols] = (num / jnp.where(lo, den[:blk], den[blk:])).astype(BF16)


def _attention(sinks, q, kd, vd, seq, q_blocks=8):
    t, nq = q.shape
    nk = kd.shape[1]
    tq = q_blocks * ATT_BLOCK
    steps = seq // tq
    batch = t // seq
    cur = lambda b, n: (b * steps + n, 0)
    prev = lambda b, n: ((b * steps + n) * q_blocks - jnp.where(n == 0, 0, 1), 0)
    return pl.pallas_call(
        functools.partial(_attn_kernel, q_blocks=q_blocks),
        grid=(batch, steps),
        in_specs=[
            pl.BlockSpec(memory_space=pltpu.SMEM),
            pl.BlockSpec((tq, nq), cur),
            pl.BlockSpec((ATT_BLOCK, nk), prev),
            pl.BlockSpec((tq, nk), cur),
            pl.BlockSpec((ATT_BLOCK, nk), prev),
            pl.BlockSpec((tq, nk), cur),
        ],
        out_specs=pl.BlockSpec((tq, nq), cur),
        out_shape=jax.ShapeDtypeStruct((t, nq), BF16),
        compiler_params=_params(("parallel", "parallel")),
        name="swa_attention",
    )(sinks, q, kd, kd, vd, vd)


def _oproj_kernel(x_ref, o_ref, gt_ref, gpost_ref, wo_ref, bo_ref, out_ref):
    gg = gt_ref[...] * gpost_ref[...]
    part = x_ref.shape[0] // ROW_PARTS
    for r in range(0, x_ref.shape[0], part):
        rows = slice(r, r + part)
        y = _dot(o_ref[rows, :], wo_ref[...]) + bo_ref[...]
        out_ref[rows, :] = x_ref[rows, :] + y * _rms_scale(y) * gg


def _oproj(x, o, gate, g_post, w_o, b_o, seq, tm=1024):
    t, d = x.shape
    nq = o.shape[1]
    tps = seq // tm
    return pl.pallas_call(
        _oproj_kernel,
        grid=(t // tm,),
        in_specs=[
            pl.BlockSpec((tm, d), lambda i: (i, 0)),
            pl.BlockSpec((tm, nq), lambda i: (i, 0)),
            pl.BlockSpec((None, 1, d), lambda i: (i // tps, 0, 0)),
            pl.BlockSpec((1, d), lambda i: (0, 0)),
            pl.BlockSpec((nq, d), lambda i: (0, 0)),
            pl.BlockSpec((1, d), lambda i: (0, 0)),
        ],
        out_specs=pl.BlockSpec((tm, d), lambda i: (i, 0)),
        out_shape=jax.ShapeDtypeStruct((t, d), F32),
        compiler_params=_params(("parallel",)),
        name="attn_out_proj",
    )(x, o, gate, g_post, w_o, b_o)


def _rope_tables(seq):
    half = ROT_DIM // 2
    inv = ROPE_THETA ** (-jnp.arange(0, ROT_DIM, 2, dtype=F32) / ROT_DIM)
    ang = jnp.arange(seq, dtype=F32)[:, None] * inv[None, :]
    cos, sin = jnp.cos(ang), jnp.sin(ang)
    ones = jnp.ones((seq, HEAD_DIM - ROT_DIM), F32)
    zeros = jnp.zeros((seq, HEAD_DIM - ROT_DIM), F32)
    zh = jnp.zeros((seq, half), F32)
    c_tab = jnp.concatenate([cos, cos, ones], axis=1)
    s1_tab = jnp.concatenate([-sin, zh, zeros], axis=1)
    s2_tab = jnp.concatenate([zh, sin, zeros], axis=1)
    rep = LANES // HEAD_DIM
    return tuple(jnp.tile(a, (1, rep)) for a in (c_tab, s1_tab, s2_tab))


def kernel(x, c, ada_w, ada_b, norm_pre, norm_post, conv_w_in, conv_w, conv_w_out,
           kv_ada_w, kv_ada_b, kv_norm, w_kv, b_kv, w_q, b_q, sinks, w_o, b_o,
           mlp_up, mlp_down):
    batch, seq, d = x.shape
    depth = ada_w.shape[0]
    n_self = conv_w_in.shape[0]
    t = batch * seq
    assert n_self == 1 and depth == 2, "one conv-mixer layer then one shared-KV attention layer"

    c_pad = jnp.pad(c, ((0, SUBLANES - batch), (0, 0)))
    mods, (w_in_bf, w_out_bf) = _ada(
        c_pad, ada_w.reshape(depth * 2, d, 3 * d), ada_b.reshape(depth * 2, 1, 3 * d),
        casts=[(conv_w_in, 0), (conv_w_out, 0)], tn=1536)
    mods = mods[:, :batch].reshape(depth, 2, batch, 3, 1, d)
    kv_mods, _ = _ada(c_pad, kv_ada_w[None], kv_ada_b[None, None])
    kv_mods = kv_mods[0, :batch].reshape(batch, 2, 1, d)

    def mod(l, sub):
        return [mods[l, sub, :, which] for which in range(3)]

    xf = x.reshape(t, d)

    xf, (up0, down0) = _mixer(
        xf, *mod(0, 0), norm_pre[0, 0:1], norm_post[0, 0:1], conv_w[0],
        w_in_bf, w_out_bf, [(mlp_up, 0), (mlp_down, 0)], seq)
    xf, (wq_bf, wo_bf, up1, down1) = _mlp(
        xf, *mod(0, 1), norm_pre[0, 1:2], norm_post[0, 1:2], up0, down0,
        [(w_q, 0), (w_o, 0), (mlp_up, 1), (mlp_down, 1)], seq)

    cos, s1, s2 = _rope_tables(seq)
    shift, scale, gate = mod(1, 0)
    q, kd, vd = _qkv(xf, shift, scale, norm_pre[1, 0:1],
                     kv_mods[:, 0], kv_mods[:, 1], kv_norm[None],
                     wq_bf, b_q[0][None], w_kv.astype(BF16), b_kv[None], cos, s1, s2, seq)
    o = _attention(sinks[0], q, kd, vd, seq)
    xf = _oproj(xf, o, gate, norm_post[1, 0:1], wo_bf, b_o[0][None], seq)
    xf, _ = _mlp(xf, *mod(1, 1), norm_pre[1, 1:2], norm_post[1, 1:2], up1, down1, [], seq)
    return xf.reshape(batch, seq, d)
```
